```python
import math
import jax, jax.numpy as jnp
from jax import lax
import numpy as np

D_MODEL = 1024
BATCH = 2
SEQ = 8192
DEPTH = 4
DEC_BATCH = 128
DEC_SEQ = 4
PAST_LEN = 2048
PAGE_SIZE = 128

N_HEADS_A = 8
HEAD_DIM_A = 64
WIDTH_A = N_HEADS_A * HEAD_DIM_A
Q_BLOCK = 128
N_GROUPS_B = 8
GROUP_DIM_B = 64
WIDTH_B = N_GROUPS_B * GROUP_DIM_B
CHUNK = 128
D_FF = -(-8 * D_MODEL // (3 * 256)) * 256
IN_WIDTH = 3 * WIDTH_A + N_HEADS_A + 2 * WIDTH_B + 2 * D_MODEL
N_ADA = 6
EPS = 1e-6
NEG_INF = -1e30

kernel_name = "fox_gmlp_gated_hybrid_step"


def rmsnorm(x, g):
    xf = x.astype(jnp.float32)
    y = xf * lax.rsqrt(jnp.mean(xf * xf, axis=-1, keepdims=True) + EPS)
    return (y * g.astype(jnp.float32)).astype(x.dtype)


def layernorm(x, g, b):
    xf = x.astype(jnp.float32)
    mu = jnp.mean(xf, axis=-1, keepdims=True)
    xc = xf - mu
    y = xc * lax.rsqrt(jnp.mean(xc * xc, axis=-1, keepdims=True) + EPS)
    return (y * g.astype(jnp.float32) + b.astype(jnp.float32)).astype(x.dtype)


def in_projection(h, w_in, b_f, q_g, k_g, ln_g, ln_b):
    B, T, _ = h.shape
    z = h @ w_in
    cuts = [WIDTH_A, 2 * WIDTH_A, 3 * WIDTH_A, 3 * WIDTH_A + N_HEADS_A,
            3 * WIDTH_A + N_HEADS_A + 2 * WIDTH_B,
            3 * WIDTH_A + N_HEADS_A + 2 * WIDTH_B + D_MODEL]
    q, k, v, f_logit, zb, ga, gb = jnp.split(z, cuts, axis=-1)
    q = rmsnorm(q.reshape(B, T, N_HEADS_A, HEAD_DIM_A), q_g)
    k = rmsnorm(k.reshape(B, T, N_HEADS_A, HEAD_DIM_A), k_g)
    v = v.reshape(B, T, N_HEADS_A, HEAD_DIM_A)
    logf = jax.nn.log_sigmoid((f_logit + b_f).astype(jnp.float32))
    zb = jax.nn.gelu(zb)
    u, vb = jnp.split(zb, 2, axis=-1)
    vb = layernorm(vb, ln_g, ln_b)
    return q, k, v, logf, u, vb, ga, gb


def fox_attend(q, k, v, F_q, F_k, q_offset):
    B, T, H, Dh = q.shape
    Sk = k.shape[1]
    qb_len = min(T, Q_BLOCK)
    nb = T // qb_len
    scale = HEAD_DIM_A ** -0.5
    qb = q.reshape(B, nb, qb_len, H, Dh).transpose(1, 0, 2, 3, 4)
    fqb = F_q.reshape(B, nb, qb_len, H).transpose(1, 0, 3, 2)
    fk = F_k.transpose(0, 2, 1)
    kpos = jnp.arange(Sk)

    def one_block(args):
        q_blk, fq, i = args
        qpos = q_offset + i * qb_len + jnp.arange(qb_len)
        s = jnp.einsum('bqhd,bkhd->bhqk', q_blk, k, preferred_element_type=jnp.float32) * scale
        s = s + fq[..., None] - fk[:, :, None, :]
        s = jnp.where(kpos[None, :] <= qpos[:, None], s, NEG_INF)
        p = jax.nn.softmax(s, axis=-1)
        return jnp.einsum('bhqk,bkhd->bqhd', p.astype(v.dtype), v)

    o = lax.map(one_block, (qb, fqb, jnp.arange(nb)))
    return o.transpose(1, 0, 2, 3, 4).reshape(B, T, H * Dh)


def spatial_gate(u, vb, w_s, b_s):
    B, T, _ = u.shape
    L = min(T, CHUNK)
    n = T // L
    w = jnp.tril(w_s[:, :L, :L])
    v5 = vb.reshape(B, n, L, N_GROUPS_B, GROUP_DIM_B)
    s = jnp.einsum('gts,bnsgc->bntgc', w, v5) + b_s[:, :L].T[None, None, :, :, None]
    return (u.reshape(B, n, L, N_GROUPS_B, GROUP_DIM_B) * s).reshape(B, T, WIDTH_B)


def decoder_layer(x, c, past, ada_w, ada_b, n1, n2, w_in, b_f, q_g, k_g, ln_g, ln_b,
                  w_s, b_s, w_a_up, w_b_up, w_out, w_ffn_in, w_ffn_out):
    mod = jax.nn.silu(c) @ ada_w + ada_b
    sh1, sc1, g1, sh2, sc2, g2 = [m[:, None, :] for m in jnp.split(mod, N_ADA, axis=-1)]
    h = rmsnorm(x, n1) * (1 + sc1) + sh1
    q, k, v, logf, u, vb, ga, gb = in_projection(h, w_in, b_f, q_g, k_g, ln_g, ln_b)
    if past is None:
        k_all, v_all, lf_all, offset = k, v, logf, 0
    else:
        k_past, v_past, lf_past = past
        k_all = jnp.concatenate([k_past.astype(k.dtype), k], axis=1)
        v_all = jnp.concatenate([v_past.astype(v.dtype), v], axis=1)
        lf_all = jnp.concatenate([lf_past.astype(jnp.float32), logf], axis=1)
        offset = k_past.shape[1]
    F = jnp.cumsum(lf_all, axis=1)
    a = fox_attend(q, k_all, v_all, F[:, offset:], F, offset)
    b = spatial_gate(u, vb, w_s, b_s)
    merged = jax.nn.sigmoid(ga) * (a @ w_a_up) + jax.nn.sigmoid(gb) * (b @ w_b_up)
    x = x + g1 * (merged @ w_out)
    h2 = rmsnorm(x, n2) * (1 + sc2) + sh2
    gate, up = jnp.split(h2 @ w_ffn_in, 2, axis=-1)
    x = x + g2 * ((jax.nn.silu(gate) * up) @ w_ffn_out)
    return x, k, v, logf, vb


def setup_inputs(seed: int = 0) -> dict:
    key = jax.random.key(seed)
    ks = jax.random.split(key, 32)
    n_pages = PAST_LEN // PAGE_SIZE
    n_used = DEC_BATCH * n_pages
    n_pool = n_used + n_used // 4
    nrm = jax.random.normal
    f32 = jnp.float32
    page_table = jax.random.permutation(ks[0], n_pool)[:n_used].reshape(DEC_BATCH, n_pages).astype(jnp.int32)
    return {
        "x_prompt": nrm(ks[1], (BATCH, SEQ, D_MODEL), f32),
        "x_sample": nrm(ks[2], (DEC_BATCH, DEC_SEQ, D_MODEL), f32),
        "cache_k": nrm(ks[3], (DEPTH, n_pool, PAGE_SIZE, N_HEADS_A, HEAD_DIM_A), f32),
        "cache_v": nrm(ks[4], (DEPTH, n_pool, PAGE_SIZE, N_HEADS_A, HEAD_DIM_A), f32),
        "cache_logf": jax.nn.log_sigmoid(2.0 + 0.5 * nrm(ks[5], (DEPTH, n_pool, PAGE_SIZE, N_HEADS_A), f32)),
        "page_table": page_table,
        "c_prompt": nrm(ks[6], (BATCH, D_MODEL), f32),
        "c_sample": nrm(ks[7], (DEC_BATCH, D_MODEL), f32),
        "ada_w": 0.5 * D_MODEL ** -0.5 * nrm(ks[8], (DEPTH, D_MODEL, N_ADA * D_MODEL), f32),
        "ada_b": 0.02 * nrm(ks[9], (DEPTH, N_ADA * D_MODEL), f32),
        "norm1_g": 1.0 + 0.05 * nrm(ks[10], (DEPTH, D_MODEL), f32),
        "norm2_g": 1.0 + 0.05 * nrm(ks[11], (DEPTH, D_MODEL), f32),
        "w_in": D_MODEL ** -0.5 * nrm(ks[12], (DEPTH, D_MODEL, IN_WIDTH), f32),
        "b_f": 2.0 + 0.5 * nrm(ks[13], (DEPTH, N_HEADS_A), f32),
        "q_norm_g": 1.0 + 0.05 * nrm(ks[14], (DEPTH, HEAD_DIM_A), f32),
        "k_norm_g": 1.0 + 0.05 * nrm(ks[15], (DEPTH, HEAD_DIM_A), f32),
        "v_ln_g": 1.0 + 0.05 * nrm(ks[16], (DEPTH, WIDTH_B), f32),
        "v_ln_b": 0.02 * nrm(ks[17], (DEPTH, WIDTH_B), f32),
        "w_spatial": CHUNK ** -0.5 * nrm(ks[18], (DEPTH, N_GROUPS_B, CHUNK, CHUNK), f32),
        "b_spatial": 1.0 + 0.1 * nrm(ks[19], (DEPTH, N_GROUPS_B, CHUNK), f32),
        "w_a_up": WIDTH_A ** -0.5 * nrm(ks[20], (DEPTH, WIDTH_A, D_MODEL), f32),
        "w_b_up": WIDTH_B ** -0.5 * nrm(ks[21], (DEPTH, WIDTH_B, D_MODEL), f32),
        "w_out": D_MODEL ** -0.5 * nrm(ks[22], (DEPTH, D_MODEL, D_MODEL), f32),
        "w_ffn_in": D_MODEL ** -0.5 * nrm(ks[23], (DEPTH, D_MODEL, 2 * D_FF), f32),
        "w_ffn_out": D_FF ** -0.5 * nrm(ks[24], (DEPTH, D_FF, D_MODEL), f32),
    }


def reference(x_prompt, x_sample, cache_k, cache_v, cache_logf, page_table, c_prompt, c_sample,
              ada_w, ada_b, norm1_g, norm2_g, w_in, b_f, q_norm_g, k_norm_g, v_ln_g, v_ln_b,
              w_spatial, b_spatial, w_a_up, w_b_up, w_out, w_ffn_in, w_ffn_out):
    n_seq, n_pages = page_table.shape
    past_len = n_pages * cache_k.shape[2]
    yp, ys = x_prompt, x_sample
    kp_l, vp_l, lp_l, ks_l, vs_l, ls_l, cv_l = [], [], [], [], [], [], []
    for l in range(DEPTH):
        w = (ada_w[l], ada_b[l], norm1_g[l], norm2_g[l], w_in[l], b_f[l], q_norm_g[l], k_norm_g[l],
             v_ln_g[l], v_ln_b[l], w_spatial[l], b_spatial[l], w_a_up[l], w_b_up[l], w_out[l],
             w_ffn_in[l], w_ffn_out[l])
        yp, kp, vp, lp, _ = decoder_layer(yp, c_prompt, None, *w)
        k_past = cache_k[l][page_table].reshape(n_seq, past_len, N_HEADS_A, HEAD_DIM_A)
        v_past = cache_v[l][page_table].reshape(n_seq, past_len, N_HEADS_A, HEAD_DIM_A)
        lf_past = cache_logf[l][page_table].reshape(n_seq, past_len, N_HEADS_A)
        ys, ks_, vs_, ls_, cv = decoder_layer(ys, c_sample, (k_past, v_past, lf_past), *w)
        kp_l.append(kp); vp_l.append(vp); lp_l.append(lp)
        ks_l.append(ks_); vs_l.append(vs_); ls_l.append(ls_); cv_l.append(cv)
    k_prompt = jnp.stack(kp_l)
    v_prompt = jnp.stack(vp_l)
    logf_prompt = jnp.stack(lp_l)
    k_sample = jnp.stack(ks_l)
    v_sample = jnp.stack(vs_l)
    logf_sample = jnp.stack(ls_l)
    chunk_v_sample = jnp.stack(cv_l)
    return (yp, ys, k_prompt, v_prompt, logf_prompt, k_sample, v_sample, logf_sample, chunk_v_sample)
```

```python
import functools
import math

import jax
import jax.numpy as jnp
from jax import lax
from jax.experimental import pallas as pl
from jax.experimental.pallas import tpu as pltpu

F32 = jnp.float32
BF16 = jnp.bfloat16

D_MODEL = 1024
DEPTH = 4
N_HEADS = 8
HEAD_DIM = 64
WIDTH = N_HEADS * HEAD_DIM
CHUNK = 128
PAGE = 128
D_FF = 2816
FF_CHUNK = 256
N_FF_CHUNKS = D_FF // FF_CHUNK
N_ADA = 6
EPS = 1e-6
NEG_INF = -1e30
SCALE = HEAD_DIM ** -0.5
LANES = 128
SUBLANES = 8
VMEM_LIMIT = 56 * 1024 * 1024

TM = 512
TQ = 512
TK = 512


def _dot(a, b):
    return jnp.dot(a, b, preferred_element_type=F32)


def _dot_nt(a, b):
    return lax.dot_general(a, b, (((1,), (1,)), ((), ())), preferred_element_type=F32)


def _split3(a):
    hi = a.astype(BF16)
    r = a - hi.astype(F32)
    mid = r.astype(BF16)
    lo = (r - mid.astype(F32)).astype(BF16)
    return hi, mid, lo


def _dot3_lhs(a, b):
    hi, mid, lo = _split3(a)
    return (_dot(lo, b) + _dot(mid, b)) + _dot(hi, b)


def _dot3_rhs(a, b):
    hi, mid, lo = _split3(b)
    return (_dot(a, lo) + _dot(a, mid)) + _dot(a, hi)


def _sigmoid(x):
    return 1.0 / (1.0 + jnp.exp(-x))


def _log_sigmoid(x):
    return jnp.minimum(x, 0.0) - jnp.log1p(jnp.exp(-jnp.abs(x)))


def _gelu_tanh(x):
    c = math.sqrt(2.0 / math.pi)
    return 0.5 * x * (1.0 + jnp.tanh(c * (x + 0.044715 * (x * x * x))))


def _rms(x):
    return x * lax.rsqrt(jnp.mean(x * x, axis=-1, keepdims=True) + EPS)


def _mod_kernel(c_ref, w_ref, b_ref, o_ref):
    c = c_ref[...]
    s = (c * _sigmoid(c)).astype(BF16)
    o_ref[0] = _dot(s, w_ref[0].astype(BF16)) + b_ref[0]


def _modulation(c_all, ada_w, ada_b):
    rows = c_all.shape[0]
    n_col = N_ADA * D_MODEL
    return pl.pallas_call(
        _mod_kernel,
        grid=(DEPTH, N_ADA),
        in_specs=[
            pl.BlockSpec((rows, D_MODEL), lambda l, n: (0, 0)),
            pl.BlockSpec((1, D_MODEL, D_MODEL), lambda l, n: (l, 0, n)),
            pl.BlockSpec((1, 1, D_MODEL), lambda l, n: (l, 0, n)),
        ],
        out_specs=pl.BlockSpec((1, rows, D_MODEL), lambda l, n: (l, 0, n)),
        out_shape=jax.ShapeDtypeStruct((DEPTH, rows, n_col), F32),
        compiler_params=pltpu.CompilerParams(vmem_limit_bytes=VMEM_LIMIT),
        name="adaln_modulation",
    )(c_all, ada_w, ada_b.reshape(DEPTH, 1, n_col))


def _inproj_common(x, m, n1_ref, wqkv_ref, wf_ref, wzb_ref, wg_ref, bf_ref, qg_ref, kg_ref,
                   lng_ref, lnb_ref, gmat_ref):
    sh1, sc1 = m[0], m[1]
    h = _rms(x) * n1_ref[0]
    h = h * (1.0 + sc1) + sh1
    hb = h.astype(BF16)

    qkv = _dot(hb, wqkv_ref[0])
    gmat = gmat_ref[...]

    def headnorm(z, g):
        sq = z * z
        hi = sq.astype(BF16)
        lo = (sq - hi.astype(F32)).astype(BF16)
        ms = _dot(hi, gmat) + _dot(lo, gmat)
        return z * lax.rsqrt(ms + EPS) * g

    q = headnorm(qkv[:, :WIDTH], qg_ref[0]) * SCALE
    k = headnorm(qkv[:, WIDTH:2 * WIDTH], kg_ref[0])
    v = qkv[:, 2 * WIDTH:]

    logf = _log_sigmoid(_dot(hb, wf_ref[0]) + bf_ref[0])

    zb = _gelu_tanh(_dot(hb, wzb_ref[0]))
    u = zb[:, :WIDTH]
    vb = zb[:, WIDTH:]
    mu = jnp.mean(vb, axis=-1, keepdims=True)
    xc = vb - mu
    vbn = xc * lax.rsqrt(jnp.mean(xc * xc, axis=-1, keepdims=True) + EPS) * lng_ref[0] + lnb_ref[0]

    g = _dot(hb, wg_ref[0])
    sga = _sigmoid(g[:, :D_MODEL])
    sgb = _sigmoid(g[:, D_MODEL:])
    return q, k, v, logf, u, vbn, sga, sgb


def _inproj_prompt_kernel(x_ref, mod_ref, n1_ref, wqkv_ref, wf_ref, wzb_ref, wg_ref, wbup_ref, bf_ref,
                          qg_ref, kg_ref, lng_ref, lnb_ref, gmat_ref, wpair_ref, bsp_ref, umat_ref,
                          kprev_ref, vprev_ref, lprev_ref,
                          q_ref, ktb_ref, vb_ref, ft_ref, sga_ref, mb_ref, kt_ref, vt_ref, lt_ref,
                          carry_ref):
    del kprev_ref, vprev_ref, lprev_ref
    tm = x_ref.shape[1]
    q, k, v, logf, u, vbn, sga, sgb = _inproj_common(
        x_ref[0], mod_ref[0], n1_ref, wqkv_ref, wf_ref, wzb_ref, wg_ref, bf_ref, qg_ref, kg_ref,
        lng_ref, lnb_ref, gmat_ref)

    q_ref[0] = q.astype(BF16)
    kt = k.T
    kt_ref[0, 0] = kt
    ktb_ref[0] = kt.astype(BF16)
    vb_ref[0] = v.astype(BF16)
    vt_ref[0, 0] = v.T

    lt = logf.T[:N_HEADS]
    lt_ref[0, 0] = lt

    @pl.when(pl.program_id(1) == 0)
    def _():
        carry_ref[...] = jnp.zeros_like(carry_ref)

    fcum = _dot3_lhs(lt, umat_ref[...]) + carry_ref[:, 0:1]
    ft_ref[0] = fcum
    carry_ref[...] = jnp.broadcast_to(fcum[:, tm - 1:tm], carry_ref.shape)

    lane = lax.broadcasted_iota(jnp.int32, (1, LANES), 1)
    left = lane < HEAD_DIM
    chunks = []
    for c in range(tm // CHUNK):
        xc = vbn[c * CHUNK:(c + 1) * CHUNK]
        cols = []
        for jp in range(WIDTH // LANES):
            xp = xc[:, jp * LANES:(jp + 1) * LANES]
            xs = jnp.concatenate([jnp.where(left, xp, 0.0), jnp.where(left, 0.0, xp)], axis=0)
            cols.append(_dot(wpair_ref[0, jp], xs.astype(BF16)))
        chunks.append(jnp.concatenate(cols, axis=1) + bsp_ref[0])
    s = jnp.concatenate(chunks, axis=0)
    bg = (u * s).astype(BF16)

    sga_ref[0] = sga.astype(BF16)
    mb_ref[0] = (sgb * _dot(bg, wbup_ref[0])).astype(BF16)


def _inproj_sample_kernel(x_ref, mod_ref, n1_ref, wqkv_ref, wf_ref, wzb_ref, wg_ref, wbup_ref, bf_ref,
                          qg_ref, kg_ref, lng_ref, lnb_ref, gmat_ref, coef_ref, bsp_ref,
                          q_ref, k_ref, v_ref, lf_ref, cv_ref, sga_ref, mb_ref):
    q, k, v, logf, u, vbn, sga, sgb = _inproj_common(
        x_ref[0], mod_ref[0], n1_ref, wqkv_ref, wf_ref, wzb_ref, wg_ref, bf_ref, qg_ref, kg_ref,
        lng_ref, lnb_ref, gmat_ref)
    q_ref[0] = q
    k_ref[0] = k
    v_ref[0] = v
    lf_ref[0] = logf
    cv_ref[0] = vbn

    s = bsp_ref[0] + coef_ref[0, 0] * vbn
    for d in range(1, coef_ref.shape[1]):
        s = s + coef_ref[0, d] * pltpu.roll(vbn, d, axis=0)
    bg = (u * s).astype(BF16)
    sga_ref[0] = sga.astype(BF16)
    mb_ref[0] = (sgb * _dot(bg, wbup_ref[0])).astype(BF16)


def _layer_spec(tail, l, nidx):
    zeros = (0,) * len(tail)
    if nidx == 1:
        imap = lambda i: (l,) + zeros
    else:
        imap = lambda b, i: (l,) + zeros
    return pl.BlockSpec((1,) + tuple(tail), imap, pipeline_mode=pl.Buffered(1))


def _const_spec(shape, nidx):
    zeros = (0,) * len(shape)
    if nidx == 1:
        imap = lambda i: zeros
    else:
        imap = lambda b, i: zeros
    return pl.BlockSpec(tuple(shape), imap, pipeline_mode=pl.Buffered(1))


def _inproj_prompt(l, x, mod, w, consts, k_all, v_all, l_all):
    B, T, _ = x.shape
    tm = TM
    grid = (B, T // tm)
    row = lambda width: pl.BlockSpec((1, tm, width), lambda b, i: (b, i, 0))
    col = lambda height: pl.BlockSpec((1, height, tm), lambda b, i: (b, 0, i))
    stacked = lambda height: pl.BlockSpec((1, 1, height, tm), lambda b, i: (l, b, 0, i))
    in_specs = [
        row(D_MODEL),
        pl.BlockSpec((1, N_ADA, 1, D_MODEL), lambda b, i: (b, 0, 0, 0)),
        _layer_spec((1, D_MODEL), l, 2),
        _layer_spec((D_MODEL, 3 * WIDTH), l, 2),
        _layer_spec((D_MODEL, LANES), l, 2),
        _layer_spec((D_MODEL, 2 * WIDTH), l, 2),
        _layer_spec((D_MODEL, 2 * D_MODEL), l, 2),
        _layer_spec((WIDTH, D_MODEL), l, 2),
        _layer_spec((1, LANES), l, 2),
        _layer_spec((1, WIDTH), l, 2),
        _layer_spec((1, WIDTH), l, 2),
        _layer_spec((1, WIDTH), l, 2),
        _layer_spec((1, WIDTH), l, 2),
        _const_spec((WIDTH, WIDTH), 2),
        _layer_spec((WIDTH // LANES, CHUNK, 2 * CHUNK), l, 2),
        _layer_spec((CHUNK, WIDTH), l, 2),
        _const_spec((tm, tm), 2),
        pl.BlockSpec(memory_space=pl.ANY),
        pl.BlockSpec(memory_space=pl.ANY),
        pl.BlockSpec(memory_space=pl.ANY),
    ]
    out_specs = [row(WIDTH), col(WIDTH), row(WIDTH), col(N_HEADS), row(D_MODEL), row(D_MODEL),
                 stacked(WIDTH), stacked(WIDTH), stacked(N_HEADS)]
    out_shape = [
        jax.ShapeDtypeStruct((B, T, WIDTH), BF16),
        jax.ShapeDtypeStruct((B, WIDTH, T), BF16),
        jax.ShapeDtypeStruct((B, T, WIDTH), BF16),
        jax.ShapeDtypeStruct((B, N_HEADS, T), F32),
        jax.ShapeDtypeStruct((B, T, D_MODEL), BF16),
        jax.ShapeDtypeStruct((B, T, D_MODEL), BF16),
        jax.ShapeDtypeStruct(k_all.shape, F32),
        jax.ShapeDtypeStruct(v_all.shape, F32),
        jax.ShapeDtypeStruct(l_all.shape, F32),
    ]
    n_in = len(in_specs)
    return pl.pallas_call(
        _inproj_prompt_kernel,
        grid=grid,
        in_specs=in_specs,
        out_specs=out_specs,
        out_shape=out_shape,
        scratch_shapes=[pltpu.VMEM((N_HEADS, LANES), F32)],
        input_output_aliases={n_in - 3: 6, n_in - 2: 7, n_in - 1: 8},
        compiler_params=pltpu.CompilerParams(
            dimension_semantics=("arbitrary", "arbitrary"), vmem_limit_bytes=VMEM_LIMIT),
        name="inproj_prompt",
    )(x, mod, w["n1"], w["wqkv"], w["wf"], w["wzb"], w["wg"], w["wbup"], w["bf"], w["qg"], w["kg"],
      w["lng"], w["lnb"], consts["gmat"], w["wpair"], w["bsp"], consts["umat"], k_all, v_all, l_all)


def _inproj_sample(l, x, mod, w, consts):
    _, R, _ = x.shape
    full = lambda width: pl.BlockSpec((1, R, width), lambda i: (0, 0, 0))
    in_specs = [
        full(D_MODEL),
        pl.BlockSpec((1, N_ADA, R, D_MODEL), lambda i: (l, 0, 0, 0), pipeline_mode=pl.Buffered(1)),
        _layer_spec((1, D_MODEL), l, 1),
        _layer_spec((D_MODEL, 3 * WIDTH), l, 1),
        _layer_spec((D_MODEL, LANES), l, 1),
        _layer_spec((D_MODEL, 2 * WIDTH), l, 1),
        _layer_spec((D_MODEL, 2 * D_MODEL), l, 1),
        _layer_spec((WIDTH, D_MODEL), l, 1),
        _layer_spec((1, LANES), l, 1),
        _layer_spec((1, WIDTH), l, 1),
        _layer_spec((1, WIDTH), l, 1),
        _layer_spec((1, WIDTH), l, 1),
        _layer_spec((1, WIDTH), l, 1),
        _const_spec((WIDTH, WIDTH), 1),
        _layer_spec(w["coef"].shape[1:], l, 1),
        _layer_spec((R, WIDTH), l, 1),
    ]
    out_specs = [full(WIDTH), full(WIDTH), full(WIDTH), full(LANES), full(WIDTH), full(D_MODEL),
                 full(D_MODEL)]
    out_shape = [
        jax.ShapeDtypeStruct((1, R, WIDTH), F32),
        jax.ShapeDtypeStruct((1, R, WIDTH), F32),
        jax.ShapeDtypeStruct((1, R, WIDTH), F32),
        jax.ShapeDtypeStruct((1, R, LANES), F32),
        jax.ShapeDtypeStruct((1, R, WIDTH), F32),
        jax.ShapeDtypeStruct((1, R, D_MODEL), BF16),
        jax.ShapeDtypeStruct((1, R, D_MODEL), BF16),
    ]
    return pl.pallas_call(
        _inproj_sample_kernel,
        grid=(1,),
        in_specs=in_specs,
        out_specs=out_specs,
        out_shape=out_shape,
        compiler_params=pltpu.CompilerParams(vmem_limit_bytes=VMEM_LIMIT),
        name="inproj_sample",
    )(x, mod, w["n1"], w["wqkv"], w["wf"], w["wzb"], w["wg"], w["wbup"], w["bf"], w["qg"], w["kg"],
      w["lng"], w["lnb"], consts["gmat"], w["coef"], w["bsp_s"])


def _attn_prompt_kernel(q_ref, kt_ref, v_ref, fk_ref, fq_ref, o_ref, m_sc, l_sc, acc_sc):
    tq = q_ref.shape[1]
    tk = v_ref.shape[1]
    i = pl.program_id(1)
    j = pl.program_id(2)

    @pl.when(j == 0)
    def _():
        m_sc[...] = jnp.full_like(m_sc, NEG_INF)
        l_sc[...] = jnp.zeros_like(l_sc)
        acc_sc[...] = jnp.zeros_like(acc_sc)

    lane = lax.broadcasted_iota(jnp.int32, (1, LANES), 1)
    left = lane < HEAD_DIM
    half_masks = (left.astype(BF16), (lane >= HEAD_DIM).astype(BF16))

    @pl.when(j <= i)
    def _():
        rows = i * tq + lax.broadcasted_iota(jnp.int32, (tq, tk), 0)
        cols = j * tk + lax.broadcasted_iota(jnp.int32, (tq, tk), 1)
        causal = cols <= rows
        fkrel = fk_ref[0] - fq_ref[0][:, 0:1]
        for jp in range(WIDTH // LANES):
            qp = q_ref[0, :, jp * LANES:(jp + 1) * LANES]
            ktp = kt_ref[0, jp * LANES:(jp + 1) * LANES, :]
            vp = v_ref[0, :, jp * LANES:(jp + 1) * LANES]
            pv = None
            alphas = []
            for half in range(2):
                h = 2 * jp + half
                s = _dot(qp * half_masks[half], ktp)
                s = s - fkrel[h:h + 1, :]
                s = jnp.where(causal, s, NEG_INF)
                m_prev = m_sc[h]
                m_new = jnp.maximum(m_prev, jnp.max(s, axis=-1, keepdims=True))
                alpha = jnp.exp(m_prev - m_new)
                p = jnp.exp(s - m_new)
                l_sc[h] = alpha * l_sc[h] + jnp.sum(p, axis=-1, keepdims=True)
                m_sc[h] = m_new
                contrib = _dot(p.astype(BF16), vp * half_masks[half])
                pv = contrib if pv is None else pv + contrib
                alphas.append(alpha)
            acc_sc[jp] = acc_sc[jp] * jnp.where(left, alphas[0], alphas[1]) + pv

    @pl.when(j == i)
    def _():
        for jp in range(WIDTH // LANES):
            inv = jnp.where(left, 1.0 / l_sc[2 * jp], 1.0 / l_sc[2 * jp + 1])
            o_ref[0, :, jp * LANES:(jp + 1) * LANES] = (acc_sc[jp] * inv).astype(o_ref.dtype)


def _attn_prompt(q, ktb, vb, ft):
    B, T, _ = q.shape
    tq, tk = TQ, TK
    grid = (B, T // tq, T // tk)
    kclamp = lambda b, i, j: (b, 0, jnp.minimum(j, i))
    return pl.pallas_call(
        _attn_prompt_kernel,
        grid=grid,
        in_specs=[
            pl.BlockSpec((1, tq, WIDTH), lambda b, i, j: (b, i, 0)),
            pl.BlockSpec((1, WIDTH, tk), kclamp),
            pl.BlockSpec((1, tk, WIDTH), lambda b, i, j: (b, jnp.minimum(j, i), 0)),
            pl.BlockSpec((1, N_HEADS, tk), kclamp),
            pl.BlockSpec((1, N_HEADS, tq), lambda b, i, j: (b, 0, i)),
        ],
        out_specs=pl.BlockSpec((1, tq, WIDTH), lambda b, i, j: (b, i, 0)),
        out_shape=jax.ShapeDtypeStruct((B, T, WIDTH), BF16),
        scratch_shapes=[
            pltpu.VMEM((N_HEADS, tq, 1), F32),
            pltpu.VMEM((N_HEADS, tq, 1), F32),
            pltpu.VMEM((WIDTH // LANES, tq, LANES), F32),
        ],
        compiler_params=pltpu.CompilerParams(
            dimension_semantics=("arbitrary", "arbitrary", "arbitrary"), vmem_limit_bytes=VMEM_LIMIT),
        name="attn_prompt",
    )(q, ktb, vb, ft, ft)


def _attn_sample_kernel(l, n_pages, pt_ref, q_ref, kn_ref, vn_ref, lfn_ref, ck_ref, cv_ref, cl_ref,
                        sl_ref, ones_ref, lsuf_ref, o_ref, kbuf, vbuf, lbuf, sem):
    b = pl.program_id(0)
    nb = pl.num_programs(0)
    slot = lax.rem(b, 2)
    n_new = q_ref.shape[1]

    def page_copies(seq, sl):
        cps = []
        for p in range(n_pages):
            page = pt_ref[seq, p]
            cps.append(pltpu.make_async_copy(ck_ref.at[l, page], kbuf.at[sl, p], sem.at[0, sl]))
            cps.append(pltpu.make_async_copy(cv_ref.at[l, page], vbuf.at[sl, p], sem.at[1, sl]))
            cps.append(pltpu.make_async_copy(cl_ref.at[l, page], lbuf.at[sl, p], sem.at[2, sl]))
        return cps

    @pl.when(b == 0)
    def _():
        for cp in page_copies(0, 0):
            cp.start()

    @pl.when(b + 1 < nb)
    def _():
        for cp in page_copies(b + 1, 1 - slot):
            cp.start()

    for cp in page_copies(b, slot):
        cp.wait()

    bd = (lax.broadcasted_iota(jnp.int32, (N_HEADS, WIDTH), 0)
          == lax.broadcasted_iota(jnp.int32, (N_HEADS, WIDTH), 1) // HEAD_DIM)

    q4 = q_ref[0]
    qe = jnp.concatenate(
        [jnp.where(bd, jnp.broadcast_to(q4[t:t + 1, :], (N_HEADS, WIDTH)), 0.0) for t in range(n_new)],
        axis=0)
    qeb = qe.astype(BF16)
    n_rows = n_new * N_HEADS

    lfn = lfn_ref[0]
    c = [lfn[:, 0:1]]
    for t in range(1, n_new):
        c.append(c[-1] + lfn[:, t:t + 1])
    cq = jnp.concatenate(c, axis=0)

    lf = lbuf[slot].reshape(n_pages * N_HEADS, PAGE)
    within = _dot3_lhs(lf, sl_ref[...])
    totals = _dot3_lhs(lf, ones_ref[...])
    decay = within + _dot3_rhs(lsuf_ref[...], totals)

    s_pages = []
    for p in range(n_pages):
        sp = _dot(qeb, kbuf[slot, p].astype(BF16))
        dp = decay[p * N_HEADS:(p + 1) * N_HEADS]
        bias = jnp.concatenate([c[t] + dp for t in range(n_new)], axis=0)
        s_pages.append(sp + bias)

    lane = lax.broadcasted_iota(jnp.int32, (n_rows, LANES), 1)
    row_t = lax.broadcasted_iota(jnp.int32, (n_rows, LANES), 0) // N_HEADS
    kn = kn_ref[0]
    vn = vn_ref[0]
    s_new = jnp.full((n_rows, LANES), NEG_INF, F32)
    for t2 in range(n_new):
        col = jnp.sum(qe * kn[t2:t2 + 1, :], axis=-1, keepdims=True)
        ck = jnp.concatenate([c[t2]] * n_new, axis=0)
        val = col + cq - ck
        s_new = jnp.where((lane == t2) & (row_t >= t2), val, s_new)

    m = jnp.max(s_new, axis=-1, keepdims=True)
    for sp in s_pages:
        m = jnp.maximum(m, jnp.max(sp, axis=-1, keepdims=True))

    p_new = jnp.exp(s_new - m)
    lsum = jnp.sum(p_new, axis=-1, keepdims=True)
    o = jnp.zeros((n_rows, WIDTH), F32)
    for t2 in range(n_new):
        o = o + p_new[:, t2:t2 + 1] * vn[t2:t2 + 1, :]
    for p in range(n_pages):
        pp = jnp.exp(s_pages[p] - m)
        lsum = lsum + jnp.sum(pp, axis=-1, keepdims=True)
        o = o + _dot_nt(pp.astype(BF16), vbuf[slot, p].astype(BF16))
    o = o / lsum

    outs = []
    for t in range(n_new):
        ot = jnp.where(bd, o[t * N_HEADS:(t + 1) * N_HEADS], 0.0)
        outs.append(jnp.sum(ot, axis=0, keepdims=True))
    o_ref[0] = jnp.concatenate(outs, axis=0)


def _attn_sample(l, page_table, q, kn, vn, lfn_t, ck_t, cv_t, cl_t, consts):
    S, n_new, _ = q.shape
    n_pages = page_table.shape[1]
    per_seq = lambda shape: pl.BlockSpec((1,) + shape, lambda b, pt: (b, 0, 0))
    const = lambda shape: pl.BlockSpec(shape, lambda b, pt: (0, 0), pipeline_mode=pl.Buffered(1))
    n_ph = n_pages * N_HEADS
    grid_spec = pltpu.PrefetchScalarGridSpec(
        num_scalar_prefetch=1,
        grid=(S,),
        in_specs=[
            per_seq((n_new, WIDTH)), per_seq((n_new, WIDTH)), per_seq((n_new, WIDTH)),
            per_seq((N_HEADS, LANES)),
            pl.BlockSpec(memory_space=pl.ANY), pl.BlockSpec(memory_space=pl.ANY),
            pl.BlockSpec(memory_space=pl.ANY),
            const((PAGE, PAGE)), const((PAGE, PAGE)), const((n_ph, n_ph)),
        ],
        out_specs=per_seq((n_new, WIDTH)),
        scratch_shapes=[
            pltpu.VMEM((2, n_pages, WIDTH, PAGE), F32),
            pltpu.VMEM((2, n_pages, WIDTH, PAGE), F32),
            pltpu.VMEM((2, n_pages, N_HEADS, PAGE), F32),
            pltpu.SemaphoreType.DMA((3, 2)),
        ],
    )
    return pl.pallas_call(
        functools.partial(_attn_sample_kernel, l, n_pages),
        grid_spec=grid_spec,
        out_shape=jax.ShapeDtypeStruct((S, n_new, WIDTH), F32),
        compiler_params=pltpu.CompilerParams(
            dimension_semantics=("arbitrary",), vmem_limit_bytes=VMEM_LIMIT),
        name="attn_sample",
    )(page_table, q, kn, vn, lfn_t, ck_t, cv_t, cl_t, consts["sl"], consts["ones"], consts["lsuf"])


def _post_kernel(x_ref, a_ref, sga_ref, mb_ref, mod_ref, n2_ref, waup_ref, wout_ref, w1_ref, w2_ref,
                 o_ref, acc_ref, h2_ref):
    m = mod_ref[0]
    g1, sh2, sc2, g2 = m[2], m[3], m[4], m[5]
    au = _dot(a_ref[0].astype(BF16), waup_ref[0])
    merged = sga_ref[0].astype(F32) * au + mb_ref[0].astype(F32)
    x1 = x_ref[0] + g1 * _dot(merged.astype(BF16), wout_ref[0])
    o_ref[0] = x1
    h2 = _rms(x1) * n2_ref[0]
    h2_ref[...] = (h2 * (1.0 + sc2) + sh2).astype(BF16)
    acc_ref[...] = jnp.zeros_like(acc_ref)

    def ffn_chunk(c, carry):
        gu = _dot(h2_ref[...], w1_ref[0, c])
        gate = gu[:, :FF_CHUNK]
        act = (gate * _sigmoid(gate) * gu[:, FF_CHUNK:]).astype(BF16)
        acc_ref[...] += _dot(act, w2_ref[0, c])
        return carry

    lax.fori_loop(0, N_FF_CHUNKS, ffn_chunk, 0)
    o_ref[0] = o_ref[0] + g2 * acc_ref[...]


def _post(l, x, a, sga, mb, mod, w, mod_rows, mod_layer_indexed):
    B, T, _ = x.shape
    tm = min(TM, T)
    grid = (B, T // tm)
    row = lambda width: pl.BlockSpec((1, tm, width), lambda b, i: (b, i, 0))
    if mod_layer_indexed:
        mod_spec = pl.BlockSpec((1, N_ADA, mod_rows, D_MODEL), lambda b, i: (l, 0, 0, 0),
                                pipeline_mode=pl.Buffered(1))
    else:
        mod_spec = pl.BlockSpec((1, N_ADA, mod_rows, D_MODEL), lambda b, i: (b, 0, 0, 0))
    return pl.pallas_call(
        _post_kernel,
        grid=grid,
        in_specs=[
            row(D_MODEL), row(WIDTH), row(D_MODEL), row(D_MODEL), mod_spec,
            _layer_spec((1, D_MODEL), l, 2),
            _layer_spec((WIDTH, D_MODEL), l, 2),
            _layer_spec((D_MODEL, D_MODEL), l, 2),
            _layer_spec((N_FF_CHUNKS, D_MODEL, 2 * FF_CHUNK), l, 2),
            _layer_spec((N_FF_CHUNKS, FF_CHUNK, D_MODEL), l, 2),
        ],
        out_specs=row(D_MODEL),
        out_shape=jax.ShapeDtypeStruct((B, T, D_MODEL), F32),
        scratch_shapes=[pltpu.VMEM((tm, D_MODEL), F32), pltpu.VMEM((tm, D_MODEL), BF16)],
        compiler_params=pltpu.CompilerParams(
            dimension_semantics=("arbitrary", "arbitrary"), vmem_limit_bytes=VMEM_LIMIT),
        name="post",
    )(x, a, sga, mb, mod, w["n2"], w["waup"], w["wout"], w["w1"], w["w2"])


def _prepare_weights(norm1_g, norm2_g, w_in, b_f, q_norm_g, k_norm_g, v_ln_g, v_ln_b, w_spatial,
                     b_spatial, w_a_up, w_b_up, w_out, w_ffn_in, w_ffn_out, n_new, n_seq):
    c0 = 3 * WIDTH
    c1 = c0 + N_HEADS
    c2 = c1 + 2 * WIDTH
    w = {}
    w["n1"] = norm1_g.reshape(DEPTH, 1, D_MODEL)
    w["n2"] = norm2_g.reshape(DEPTH, 1, D_MODEL)
    w["wqkv"] = w_in[:, :, :c0].astype(BF16)
    w["wf"] = jnp.pad(w_in[:, :, c0:c1], ((0, 0), (0, 0), (0, LANES - N_HEADS))).astype(BF16)
    w["wzb"] = w_in[:, :, c1:c2].astype(BF16)
    w["wg"] = w_in[:, :, c2:].astype(BF16)
    w["bf"] = jnp.pad(b_f, ((0, 0), (0, LANES - N_HEADS))).reshape(DEPTH, 1, LANES)
    w["qg"] = jnp.tile(q_norm_g, (1, N_HEADS)).reshape(DEPTH, 1, WIDTH)
    w["kg"] = jnp.tile(k_norm_g, (1, N_HEADS)).reshape(DEPTH, 1, WIDTH)
    w["lng"] = v_ln_g.reshape(DEPTH, 1, WIDTH)
    w["lnb"] = v_ln_b.reshape(DEPTH, 1, WIDTH)
    w["waup"] = w_a_up.astype(BF16)
    w["wbup"] = w_b_up.astype(BF16)
    w["wout"] = w_out.astype(BF16)
    gate = w_ffn_in[:, :, :D_FF].reshape(DEPTH, D_MODEL, N_FF_CHUNKS, FF_CHUNK)
    up = w_ffn_in[:, :, D_FF:].reshape(DEPTH, D_MODEL, N_FF_CHUNKS, FF_CHUNK)
    w["w1"] = jnp.concatenate([gate, up], axis=-1).transpose(0, 2, 1, 3).astype(BF16)
    w["w2"] = w_ffn_out.reshape(DEPTH, N_FF_CHUNKS, FF_CHUNK, D_MODEL).astype(BF16)
    wt = jnp.tril(w_spatial)
    wt = wt.reshape(DEPTH, WIDTH // LANES, 2, CHUNK, CHUNK)
    w["wpair"] = jnp.concatenate([wt[:, :, 0], wt[:, :, 1]], axis=-1).astype(BF16)
    w["bsp"] = jnp.repeat(b_spatial.transpose(0, 2, 1), HEAD_DIM, axis=2)
    ws = jnp.tril(w_spatial[:, :, :n_new, :n_new])
    coefs = []
    for d in range(n_new):
        diag = jnp.stack([ws[:, :, t, t - d] if t >= d else jnp.zeros_like(ws[:, :, 0, 0])
                          for t in range(n_new)], axis=1)
        coefs.append(jnp.tile(jnp.repeat(diag, HEAD_DIM, axis=2), (1, n_seq, 1)))
    w["coef"] = jnp.stack(coefs, axis=1)
    bs = jnp.repeat(b_spatial[:, :, :n_new].transpose(0, 2, 1), HEAD_DIM, axis=2)
    w["bsp_s"] = jnp.tile(bs, (1, n_seq, 1))
    return w


def _constants(n_pages):
    r = jnp.arange(WIDTH)
    gmat = jnp.where((r[:, None] // HEAD_DIM) == (r[None, :] // HEAD_DIM), 1.0 / HEAD_DIM, 0.0)
    t = jnp.arange(TM)
    umat = (t[:, None] <= t[None, :])
    p = jnp.arange(PAGE)
    sl = (p[:, None] > p[None, :])
    ph = jnp.arange(n_pages * N_HEADS)
    lsuf = ((ph[:, None] % N_HEADS) == (ph[None, :] % N_HEADS)) & (
        (ph[None, :] // N_HEADS) > (ph[:, None] // N_HEADS))
    return {
        "gmat": gmat.astype(BF16),
        "umat": umat.astype(BF16),
        "sl": sl.astype(BF16),
        "ones": jnp.ones((PAGE, PAGE), BF16),
        "lsuf": lsuf.astype(BF16),
    }


def kernel(x_prompt, x_sample, cache_k, cache_v, cache_logf, page_table, c_prompt, c_sample, ada_w, ada_b, norm1_g, norm2_g, w_in, b_f, q_norm_g, k_norm_g, v_ln_g, v_ln_b, w_spatial, b_spatial, w_a_up, w_b_up, w_out, w_ffn_in, w_ffn_out):
    B, T, _ = x_prompt.shape
    S, n_new, _ = x_sample.shape
    n_pages = page_table.shape[1]
    R = S * n_new

    w = _prepare_weights(norm1_g, norm2_g, w_in, b_f, q_norm_g, k_norm_g, v_ln_g, v_ln_b, w_spatial,
                         b_spatial, w_a_up, w_b_up, w_out, w_ffn_in, w_ffn_out, n_new, S)
    consts = _constants(n_pages)

    n_c = B + S
    pad = (-n_c) % SUBLANES
    c_all = jnp.concatenate([c_prompt, c_sample, jnp.zeros((pad, D_MODEL), F32)], axis=0)
    mod = _modulation(c_all, ada_w, ada_b)
    mod_p = mod[:, :B].reshape(DEPTH, B, N_ADA, 1, D_MODEL)
    mod_s = jnp.repeat(mod[:, B:n_c].reshape(DEPTH, S, N_ADA, D_MODEL), n_new, axis=1)
    mod_s = mod_s.transpose(0, 2, 1, 3)

    ck_t = cache_k.transpose(0, 1, 3, 4, 2).reshape(DEPTH, -1, WIDTH, PAGE)
    cv_t = cache_v.transpose(0, 1, 3, 4, 2).reshape(DEPTH, -1, WIDTH, PAGE)
    cl_t = cache_logf.transpose(0, 1, 3, 2)

    k_all = jnp.zeros((DEPTH, B, WIDTH, T), F32)
    v_all = jnp.zeros((DEPTH, B, WIDTH, T), F32)
    l_all = jnp.zeros((DEPTH, B, N_HEADS, T), F32)

    yp = x_prompt
    ys = x_sample.reshape(1, R, D_MODEL)
    ks_l, vs_l, ls_l, cv_l = [], [], [], []
    for l in range(DEPTH):
        q, ktb, vb, ft, sga, mb, k_all, v_all, l_all = _inproj_prompt(
            l, yp, mod_p[l], w, consts, k_all, v_all, l_all)
        a = _attn_prompt(q, ktb, vb, ft)
        yp = _post(l, yp, a, sga, mb, mod_p[l], w, 1, False)

        qs, kn, vn, lfn, cvs, sga_s, mb_s = _inproj_sample(l, ys, mod_s, w, consts)
        lfn_t = lfn[0, :, :N_HEADS].reshape(S, n_new, N_HEADS).transpose(0, 2, 1)
        lfn_t = jnp.pad(lfn_t, ((0, 0), (0, 0), (0, LANES - n_new)))
        a_s = _attn_sample(l, page_table, qs.reshape(S, n_new, WIDTH), kn.reshape(S, n_new, WIDTH),
                           vn.reshape(S, n_new, WIDTH), lfn_t, ck_t, cv_t, cl_t, consts)
        ys = _post(l, ys, a_s.reshape(1, R, WIDTH), sga_s, mb_s, mod_s, w, R, True)

        ks_l.append(kn.reshape(S, n_new, N_HEADS, HEAD_DIM))
        vs_l.append(vn.reshape(S, n_new, N_HEADS, HEAD_DIM))
        ls_l.append(lfn[0, :, :N_HEADS].reshape(S, n_new, N_HEADS))
        cv_l.append(cvs.reshape(S, n_new, WIDTH))

    k_prompt = k_all.reshape(DEPTH, B, N_HEADS, HEAD_DIM, T).transpose(0, 1, 4, 2, 3)
    v_prompt = v_all.reshape(DEPTH, B, N_HEADS, HEAD_DIM, T).transpose(0, 1, 4, 2, 3)
    logf_prompt = l_all.transpose(0, 1, 3, 2)
    return (yp, ys.reshape(S, n_new, D_MODEL), k_prompt, v_prompt, logf_prompt,
            jnp.stack(ks_l), jnp.stack(vs_l), jnp.stack(ls_l), jnp.stack(cv_l))
```

```python
import functools
import math

import jax
import jax.numpy as jnp
from jax import lax
from jax.experimental import pallas as pl
from jax.experimental.pallas import tpu as pltpu

F32 = jnp.float32
BF16 = jnp.bfloat16

D_MODEL = 1024
DEPTH = 4
N_HEADS = 8
HEAD_DIM = 64
WIDTH = N_HEADS * HEAD_DIM
CHUNK = 128
PAGE = 128
D_FF = 2816
FF_CHUNK = 256
N_FF_CHUNKS = D_FF // FF_CHUNK
N_ADA = 6
EPS = 1e-6
NEG_INF = -1e30
SCALE = HEAD_DIM ** -0.5
LOG2E = math.log2(math.e)
ZERO_WEIGHT_EXPONENT = -105.0
LANES = 128
SUBLANES = 8
VMEM_LIMIT = 56 * 1024 * 1024

TM = 512
TQ = 256
PAIRS_PER_LOOP = 4


def _dot(a, b):
    return jnp.dot(a, b, preferred_element_type=F32)


def _dot_nt(a, b):
    return lax.dot_general(a, b, (((1,), (1,)), ((), ())), preferred_element_type=F32)


def _split3(a):
    hi = a.astype(BF16)
    r = a - hi.astype(F32)
    mid = r.astype(BF16)
    lo = (r - mid.astype(F32)).astype(BF16)
    return hi, mid, lo


def _dot3_lhs(a, b):
    hi, mid, lo = _split3(a)
    return (_dot(lo, b) + _dot(mid, b)) + _dot(hi, b)


def _dot3_rhs(a, b):
    hi, mid, lo = _split3(b)
    return (_dot(a, lo) + _dot(a, mid)) + _dot(a, hi)


def _sigmoid(x):
    return 1.0 / (1.0 + jnp.exp(-x))


def _log_sigmoid(x):
    return jnp.minimum(x, 0.0) - jnp.log1p(jnp.exp(-jnp.abs(x)))


def _gelu_tanh(x):
    c = math.sqrt(2.0 / math.pi)
    return 0.5 * x * (1.0 + jnp.tanh(c * (x + 0.044715 * (x * x * x))))


def _rms(x):
    return x * lax.rsqrt(jnp.mean(x * x, axis=-1, keepdims=True) + EPS)


def _mod_kernel(c_ref, w_ref, b_ref, o_ref):
    c = c_ref[...]
    s = (c * _sigmoid(c)).astype(BF16)
    o_ref[0] = _dot(s, w_ref[0].astype(BF16)) + b_ref[0]


def _modulation(c_all, ada_w, ada_b):
    rows = c_all.shape[0]
    n_col = N_ADA * D_MODEL
    return pl.pallas_call(
        _mod_kernel,
        grid=(DEPTH, N_ADA),
        in_specs=[
            pl.BlockSpec((rows, D_MODEL), lambda l, n: (0, 0)),
            pl.BlockSpec((1, D_MODEL, D_MODEL), lambda l, n: (l, 0, n)),
            pl.BlockSpec((1, 1, D_MODEL), lambda l, n: (l, 0, n)),
        ],
        out_specs=pl.BlockSpec((1, rows, D_MODEL), lambda l, n: (l, 0, n)),
        out_shape=jax.ShapeDtypeStruct((DEPTH, rows, n_col), F32),
        compiler_params=pltpu.CompilerParams(vmem_limit_bytes=VMEM_LIMIT),
        name="adaln_modulation",
    )(c_all, ada_w, ada_b.reshape(DEPTH, 1, n_col))


def _inproj_common(x, m, n1_ref, wqkv_ref, wf_ref, wzb_ref, wg_ref, bf_ref, qg_ref, kg_ref,
                   lng_ref, lnb_ref, gmat_ref, q_scale):
    sh1, sc1 = m[0], m[1]
    h = _rms(x) * n1_ref[0]
    h = h * (1.0 + sc1) + sh1
    hb = h.astype(BF16)

    qkv = _dot(hb, wqkv_ref[0])
    gmat = gmat_ref[...]

    def headnorm(z, g):
        sq = z * z
        hi = sq.astype(BF16)
        lo = (sq - hi.astype(F32)).astype(BF16)
        ms = _dot(hi, gmat) + _dot(lo, gmat)
        return z * lax.rsqrt(ms + EPS) * g

    q = headnorm(qkv[:, :WIDTH], qg_ref[0]) * q_scale
    k = headnorm(qkv[:, WIDTH:2 * WIDTH], kg_ref[0])
    v = qkv[:, 2 * WIDTH:]

    logf = _log_sigmoid(_dot(hb, wf_ref[0]) + bf_ref[0])

    zb = _gelu_tanh(_dot(hb, wzb_ref[0]))
    u = zb[:, :WIDTH]
    vb = zb[:, WIDTH:]
    mu = jnp.mean(vb, axis=-1, keepdims=True)
    xc = vb - mu
    vbn = xc * lax.rsqrt(jnp.mean(xc * xc, axis=-1, keepdims=True) + EPS) * lng_ref[0] + lnb_ref[0]

    g = _dot(hb, wg_ref[0])
    sga = _sigmoid(g[:, :D_MODEL])
    sgb = _sigmoid(g[:, D_MODEL:])
    return q, k, v, logf, u, vbn, sga, sgb


def _inproj_prompt_kernel(x_ref, mod_ref, n1_ref, wqkv_ref, wf_ref, wzb_ref, wg_ref, wbup_ref, bf_ref,
                          qg_ref, kg_ref, lng_ref, lnb_ref, gmat_ref, wpair_ref, bsp_ref, umat_ref,
                          ltri_ref, kprev_ref, vprev_ref, lprev_ref,
                          qt_ref, kb_ref, vtb_ref, ft_ref, fb_ref, sga_ref, mb_ref, kt_ref, vt_ref, lt_ref,
                          carry_ref, carry_row_ref):
    del kprev_ref, vprev_ref, lprev_ref
    tm = x_ref.shape[1]
    q, k, v, logf, u, vbn, sga, sgb = _inproj_common(
        x_ref[0], mod_ref[0], n1_ref, wqkv_ref, wf_ref, wzb_ref, wg_ref, bf_ref, qg_ref, kg_ref,
        lng_ref, lnb_ref, gmat_ref, SCALE * LOG2E)

    qt_ref[0] = q.T.astype(BF16)
    kt_ref[0, 0] = k.T
    kb_ref[0] = k.astype(BF16)
    vt = v.T
    vt_ref[0, 0] = vt
    vtb_ref[0] = vt.astype(BF16)

    lt = logf.T[:N_HEADS]
    lt_ref[0, 0] = lt

    @pl.when(pl.program_id(1) == 0)
    def _():
        carry_ref[...] = jnp.zeros_like(carry_ref)
        carry_row_ref[...] = jnp.zeros_like(carry_row_ref)

    fcum = _dot3_lhs(lt, umat_ref[...]) + carry_ref[:, 0:1]
    ft_ref[0] = fcum
    carry_ref[...] = jnp.broadcast_to(fcum[:, tm - 1:tm], carry_ref.shape)

    frow = _dot3_rhs(ltri_ref[...], logf) + carry_row_ref[0:1, :]
    carry_row_ref[...] = jnp.broadcast_to(frow[tm - 1:tm, :], carry_row_ref.shape)
    hi, mid, lo = _split3(frow * (-LOG2E))
    lane_id = lax.broadcasted_iota(jnp.int32, (1, LANES), 1)
    piece = jnp.where(lane_id < 2 * SUBLANES, hi.astype(F32),
                      jnp.where(lane_id < 4 * SUBLANES, mid.astype(F32), lo.astype(F32)))
    used = (lane_id % (2 * SUBLANES) < N_HEADS) & (lane_id < 6 * SUBLANES)
    fb_ref[0] = jnp.where(used, piece, 0.0).astype(BF16)

    lane = lax.broadcasted_iota(jnp.int32, (1, LANES), 1)
    left = lane < HEAD_DIM
    chunks = []
    for c in range(tm // CHUNK):
        xc = vbn[c * CHUNK:(c + 1) * CHUNK]
        cols = []
        for jp in range(WIDTH // LANES):
            xp = xc[:, jp * LANES:(jp + 1) * LANES]
            xs = jnp.concatenate([jnp.where(left, xp, 0.0), jnp.where(left, 0.0, xp)], axis=0)
            cols.append(_dot(wpair_ref[0, jp], xs.astype(BF16)))
        chunks.append(jnp.concatenate(cols, axis=1) + bsp_ref[0])
    s = jnp.concatenate(chunks, axis=0)
    bg = (u * s).astype(BF16)

    sga_ref[0] = sga.astype(BF16)
    mb_ref[0] = (sgb * _dot(bg, wbup_ref[0])).astype(BF16)


def _inproj_sample_kernel(x_ref, mod_ref, n1_ref, wqkv_ref, wf_ref, wzb_ref, wg_ref, wbup_ref, bf_ref,
                          qg_ref, kg_ref, lng_ref, lnb_ref, gmat_ref, coef_ref, bsp_ref,
                          q_ref, k_ref, v_ref, lf_ref, cv_ref, sga_ref, mb_ref):
    q, k, v, logf, u, vbn, sga, sgb = _inproj_common(
        x_ref[0], mod_ref[0], n1_ref, wqkv_ref, wf_ref, wzb_ref, wg_ref, bf_ref, qg_ref, kg_ref,
        lng_ref, lnb_ref, gmat_ref, SCALE)
    q_ref[0] = q
    k_ref[0] = k
    v_ref[0] = v
    lf_ref[0] = logf
    cv_ref[0] = vbn

    s = bsp_ref[0] + coef_ref[0, 0] * vbn
    for d in range(1, coef_ref.shape[1]):
        s = s + coef_ref[0, d] * pltpu.roll(vbn, d, axis=0)
    bg = (u * s).astype(BF16)
    sga_ref[0] = sga.astype(BF16)
    mb_ref[0] = (sgb * _dot(bg, wbup_ref[0])).astype(BF16)


def _layer_spec(tail, l, nidx):
    zeros = (0,) * len(tail)
    if nidx == 1:
        imap = lambda i: (l,) + zeros
    else:
        imap = lambda b, i: (l,) + zeros
    return pl.BlockSpec((1,) + tuple(tail), imap, pipeline_mode=pl.Buffered(1))


def _const_spec(shape, nidx):
    zeros = (0,) * len(shape)
    if nidx == 1:
        imap = lambda i: zeros
    else:
        imap = lambda b, i: zeros
    return pl.BlockSpec(tuple(shape), imap, pipeline_mode=pl.Buffered(1))


def _inproj_prompt(l, x, mod, w, consts, k_all, v_all, l_all):
    B, T, _ = x.shape
    tm = TM
    grid = (B, T // tm)
    row = lambda width: pl.BlockSpec((1, tm, width), lambda b, i: (b, i, 0))
    col = lambda height: pl.BlockSpec((1, height, tm), lambda b, i: (b, 0, i))
    stacked = lambda height: pl.BlockSpec((1, 1, height, tm), lambda b, i: (l, b, 0, i))
    in_specs = [
        row(D_MODEL),
        pl.BlockSpec((1, N_ADA, 1, D_MODEL), lambda b, i: (b, 0, 0, 0)),
        _layer_spec((1, D_MODEL), l, 2),
        _layer_spec((D_MODEL, 3 * WIDTH), l, 2),
        _layer_spec((D_MODEL, LANES), l, 2),
        _layer_spec((D_MODEL, 2 * WIDTH), l, 2),
        _layer_spec((D_MODEL, 2 * D_MODEL), l, 2),
        _layer_spec((WIDTH, D_MODEL), l, 2),
        _layer_spec((1, LANES), l, 2),
        _layer_spec((1, WIDTH), l, 2),
        _layer_spec((1, WIDTH), l, 2),
        _layer_spec((1, WIDTH), l, 2),
        _layer_spec((1, WIDTH), l, 2),
        _const_spec((WIDTH, WIDTH), 2),
        _layer_spec((WIDTH // LANES, CHUNK, 2 * CHUNK), l, 2),
        _layer_spec((CHUNK, WIDTH), l, 2),
        _const_spec((tm, tm), 2),
        _const_spec((tm, tm), 2),
        pl.BlockSpec(memory_space=pl.ANY),
        pl.BlockSpec(memory_space=pl.ANY),
        pl.BlockSpec(memory_space=pl.ANY),
    ]
    out_specs = [col(WIDTH), row(WIDTH), col(WIDTH), col(N_HEADS), row(LANES), row(D_MODEL),
                 row(D_MODEL), stacked(WIDTH), stacked(WIDTH), stacked(N_HEADS)]
    out_shape = [
        jax.ShapeDtypeStruct((B, WIDTH, T), BF16),
        jax.ShapeDtypeStruct((B, T, WIDTH), BF16),
        jax.ShapeDtypeStruct((B, WIDTH, T), BF16),
        jax.ShapeDtypeStruct((B, N_HEADS, T), F32),
        jax.ShapeDtypeStruct((B, T, LANES), BF16),
        jax.ShapeDtypeStruct((B, T, D_MODEL), BF16),
        jax.ShapeDtypeStruct((B, T, D_MODEL), BF16),
        jax.ShapeDtypeStruct(k_all.shape, F32),
        jax.ShapeDtypeStruct(v_all.shape, F32),
        jax.ShapeDtypeStruct(l_all.shape, F32),
    ]
    n_in = len(in_specs)
    return pl.pallas_call(
        _inproj_prompt_kernel,
        grid=grid,
        in_specs=in_specs,
        out_specs=out_specs,
        out_shape=out_shape,
        scratch_shapes=[pltpu.VMEM((N_HEADS, LANES), F32), pltpu.VMEM((SUBLANES, LANES), F32)],
        input_output_aliases={n_in - 3: 7, n_in - 2: 8, n_in - 1: 9},
        compiler_params=pltpu.CompilerParams(
            dimension_semantics=("arbitrary", "arbitrary"), vmem_limit_bytes=VMEM_LIMIT),
        name="inproj_prompt",
    )(x, mod, w["n1"], w["wqkv"], w["wf"], w["wzb"], w["wg"], w["wbup"], w["bf"], w["qg"], w["kg"],
      w["lng"], w["lnb"], consts["gmat"], w["wpair"], w["bsp"], consts["umat"], consts["ltri"],
      k_all, v_all, l_all)


def _inproj_sample(l, x, mod, w, consts):
    _, R, _ = x.shape
    full = lambda width: pl.BlockSpec((1, R, width), lambda i: (0, 0, 0))
    in_specs = [
        full(D_MODEL),
        pl.BlockSpec((1, N_ADA, R, D_MODEL), lambda i: (l, 0, 0, 0), pipeline_mode=pl.Buffered(1)),
        _layer_spec((1, D_MODEL), l, 1),
        _layer_spec((D_MODEL, 3 * WIDTH), l, 1),
        _layer_spec((D_MODEL, LANES), l, 1),
        _layer_spec((D_MODEL, 2 * WIDTH), l, 1),
        _layer_spec((D_MODEL, 2 * D_MODEL), l, 1),
        _layer_spec((WIDTH, D_MODEL), l, 1),
        _layer_spec((1, LANES), l, 1),
        _layer_spec((1, WIDTH), l, 1),
        _layer_spec((1, WIDTH), l, 1),
        _layer_spec((1, WIDTH), l, 1),
        _layer_spec((1, WIDTH), l, 1),
        _const_spec((WIDTH, WIDTH), 1),
        _layer_spec(w["coef"].shape[1:], l, 1),
        _layer_spec((R, WIDTH), l, 1),
    ]
    out_specs = [full(WIDTH), full(WIDTH), full(WIDTH), full(LANES), full(WIDTH), full(D_MODEL),
                 full(D_MODEL)]
    out_shape = [
        jax.ShapeDtypeStruct((1, R, WIDTH), F32),
        jax.ShapeDtypeStruct((1, R, WIDTH), F32),
        jax.ShapeDtypeStruct((1, R, WIDTH), F32),
        jax.ShapeDtypeStruct((1, R, LANES), F32),
        jax.ShapeDtypeStruct((1, R, WIDTH), F32),
        jax.ShapeDtypeStruct((1, R, D_MODEL), BF16),
        jax.ShapeDtypeStruct((1, R, D_MODEL), BF16),
    ]
    return pl.pallas_call(
        _inproj_sample_kernel,
        grid=(1,),
        in_specs=in_specs,
        out_specs=out_specs,
        out_shape=out_shape,
        compiler_params=pltpu.CompilerParams(vmem_limit_bytes=VMEM_LIMIT),
        name="inproj_sample",
    )(x, mod, w["n1"], w["wqkv"], w["wf"], w["wzb"], w["wg"], w["wbup"], w["bf"], w["qg"], w["kg"],
      w["lng"], w["lnb"], consts["gmat"], w["coef"], w["bsp_s"])


def _attn_prompt_kernel(jlo_ref, qt_ref, k_ref, fb_ref, vt_ref, o_ref, qa_sc, m_sc, acc_sc):
    b = pl.program_id(0)
    i = pl.program_id(1)
    tq = qt_ref.shape[2]
    sub = lax.broadcasted_iota(jnp.int32, (LANES, tq), 0)
    key_ix = lax.broadcasted_iota(jnp.int32, (tq, tq), 0)
    qry_ix = lax.broadcasted_iota(jnp.int32, (tq, tq), 1)

    keeps, ones_rows = [], []
    for half in range(2):
        mine = (sub // HEAD_DIM) == half
        keeps.append(jnp.where(mine, 1.0, 0.0).astype(BF16))
        l_row = HEAD_DIM if half == 0 else 0
        ones_rows.append(jnp.where(sub == l_row, 1.0, 0.0).astype(BF16))

    n_pairs = WIDTH // LANES
    for g in range(n_pairs // PAIRS_PER_LOOP):
        pairs = range(g * PAIRS_PER_LOOP, (g + 1) * PAIRS_PER_LOOP)
        jlo = i
        for jp in pairs:
            qtp = qt_ref[0, jp * LANES:(jp + 1) * LANES, :].astype(F32)
            for half in range(2):
                h = 2 * jp + half
                slot = h - 2 * pairs[0]
                mine = (sub // HEAD_DIM) == half
                sel = (sub == h) | (sub == 2 * SUBLANES + h) | (sub == 4 * SUBLANES + h)
                qa_sc[slot] = jnp.concatenate(
                    [jnp.where(mine, qtp, 0.0), jnp.where(sel, 1.0, 0.0)], axis=0).astype(BF16)
                m_sc[slot] = jnp.full((1, tq), NEG_INF, F32)
                acc_sc[slot] = jnp.zeros((LANES, tq), F32)
                jlo = jnp.minimum(jlo, jlo_ref[b, i, h])

        def block(j, diagonal, pairs=pairs):
            off = pl.multiple_of(j * tq, tq)
            fbj = fb_ref[0, pl.ds(off, tq), :]
            scores = []
            for jp in pairs:
                ka = jnp.concatenate(
                    [k_ref[0, pl.ds(off, tq), jp * LANES:(jp + 1) * LANES], fbj], axis=1)
                for half in range(2):
                    scores.append(_dot(ka, qa_sc[2 * (jp - pairs[0]) + half]))
            probs, alphas = [], []
            for slot, s in enumerate(scores):
                if diagonal:
                    s = jnp.where(key_ix <= qry_ix, s, NEG_INF)
                m_prev = m_sc[slot]
                m_new = jnp.maximum(m_prev, jnp.max(s, axis=0, keepdims=True))
                alphas.append(jnp.exp2(m_prev - m_new))
                probs.append(jnp.exp2(s - m_new).astype(BF16))
                m_sc[slot] = m_new
            for jp in pairs:
                vt = vt_ref[0, jp * LANES:(jp + 1) * LANES, pl.ds(off, tq)]
                for half in range(2):
                    slot = 2 * (jp - pairs[0]) + half
                    va = vt * keeps[half] + ones_rows[half]
                    acc_sc[slot] = alphas[slot] * acc_sc[slot] + _dot(va, probs[slot])

        def off_diagonal(j, carry, block=block):
            block(j, False)
            return carry

        lax.fori_loop(jlo, i, off_diagonal, 0)
        block(i, True)

        for jp in pairs:
            a0 = acc_sc[2 * (jp - pairs[0])]
            a1 = acc_sc[2 * (jp - pairs[0]) + 1]
            out_t = jnp.concatenate(
                [a0[:HEAD_DIM] * (1.0 / a0[HEAD_DIM:HEAD_DIM + 1]), a1[HEAD_DIM:] * (1.0 / a1[0:1])],
                axis=0)
            o_ref[0, :, jp * LANES:(jp + 1) * LANES] = out_t.T.astype(o_ref.dtype)


def _first_key_block(ft, qg, kg, tq):
    B, H, T = ft.shape
    n = T // tq
    qk_bound = HEAD_DIM * SCALE * jnp.max(jnp.abs(qg)) * jnp.max(jnp.abs(kg))
    f_first = ft[:, :, 0::tq]
    f_last = ft[:, :, tq - 1::tq]
    exponent = 2.0 * qk_bound + f_first[:, :, :, None] - f_last[:, :, None, :]
    earlier = jnp.arange(n)[None, :] < jnp.arange(n)[:, None]
    dead = (exponent < ZERO_WEIGHT_EXPONENT) & earlier[None, None]
    jlo = jnp.sum(jnp.cumprod(dead.astype(jnp.int32), axis=-1), axis=-1)
    return jlo.transpose(0, 2, 1).astype(jnp.int32)


def _attn_prompt(qt, kb, fb, vtb, jlo):
    B, T, _ = kb.shape
    tq = TQ
    resident = lambda shape: pl.BlockSpec(shape, lambda b, i, jl: (b, 0, 0), pipeline_mode=pl.Buffered(1))
    grid_spec = pltpu.PrefetchScalarGridSpec(
        num_scalar_prefetch=1,
        grid=(B, T // tq),
        in_specs=[
            pl.BlockSpec((1, WIDTH, tq), lambda b, i, jl: (b, 0, i)),
            resident((1, T, WIDTH)),
            resident((1, T, LANES)),
            resident((1, WIDTH, T)),
        ],
        out_specs=pl.BlockSpec((1, tq, WIDTH), lambda b, i, jl: (b, i, 0)),
        scratch_shapes=[
            pltpu.VMEM((2 * PAIRS_PER_LOOP, 2 * LANES, tq), BF16),
            pltpu.VMEM((2 * PAIRS_PER_LOOP, 1, tq), F32),
            pltpu.VMEM((2 * PAIRS_PER_LOOP, LANES, tq), F32),
        ],
    )
    return pl.pallas_call(
        _attn_prompt_kernel,
        grid_spec=grid_spec,
        out_shape=jax.ShapeDtypeStruct((B, T, WIDTH), BF16),
        compiler_params=pltpu.CompilerParams(
            dimension_semantics=("arbitrary", "arbitrary"), vmem_limit_bytes=VMEM_LIMIT),
        name="attn_prompt",
    )(jlo, qt, kb, fb, vtb)


def _attn_sample_kernel(l, n_pages, pt_ref, q_ref, kn_ref, vn_ref, lfn_ref, ck_ref, cv_ref, cl_ref,
                        sl_ref, ones_ref, lsuf_ref, o_ref, kbuf, vbuf, lbuf, sem):
    b = pl.program_id(0)
    nb = pl.num_programs(0)
    slot = lax.rem(b, 2)
    n_new = q_ref.shape[1]

    def page_copies(seq, sl):
        cps = []
        for p in range(n_pages):
            page = pt_ref[seq, p]
            cps.append(pltpu.make_async_copy(ck_ref.at[l, page], kbuf.at[sl, p], sem.at[0, sl]))
            cps.append(pltpu.make_async_copy(cv_ref.at[l, page], vbuf.at[sl, p], sem.at[1, sl]))
            cps.append(pltpu.make_async_copy(cl_ref.at[l, page], lbuf.at[sl, p], sem.at[2, sl]))
        return cps

    @pl.when(b == 0)
    def _():
        for cp in page_copies(0, 0):
            cp.start()

    @pl.when(b + 1 < nb)
    def _():
        for cp in page_copies(b + 1, 1 - slot):
            cp.start()

    for cp in page_copies(b, slot):
        cp.wait()

    bd = (lax.broadcasted_iota(jnp.int32, (N_HEADS, WIDTH), 0)
          == lax.broadcasted_iota(jnp.int32, (N_HEADS, WIDTH), 1) // HEAD_DIM)

    q4 = q_ref[0]
    qe = jnp.concatenate(
        [jnp.where(bd, jnp.broadcast_to(q4[t:t + 1, :], (N_HEADS, WIDTH)), 0.0) for t in range(n_new)],
        axis=0)
    qeb = qe.astype(BF16)
    n_rows = n_new * N_HEADS

    lfn = lfn_ref[0]
    c = [lfn[:, 0:1]]
    for t in range(1, n_new):
        c.append(c[-1] + lfn[:, t:t + 1])
    cq = jnp.concatenate(c, axis=0)

    lf = lbuf[slot].reshape(n_pages * N_HEADS, PAGE)
    within = _dot3_lhs(lf, sl_ref[...])
    totals = _dot3_lhs(lf, ones_ref[...])
    decay = within + _dot3_rhs(lsuf_ref[...], totals)

    s_pages = []
    for p in range(n_pages):
        sp = _dot(qeb, kbuf[slot, p].astype(BF16))
        dp = decay[p * N_HEADS:(p + 1) * N_HEADS]
        bias = jnp.concatenate([c[t] + dp for t in range(n_new)], axis=0)
        s_pages.append(sp + bias)

    lane = lax.broadcasted_iota(jnp.int32, (n_rows, LANES), 1)
    row_t = lax.broadcasted_iota(jnp.int32, (n_rows, LANES), 0) // N_HEADS
    kn = kn_ref[0]
    vn = vn_ref[0]
    s_new = jnp.full((n_rows, LANES), NEG_INF, F32)
    for t2 in range(n_new):
        col = jnp.sum(qe * kn[t2:t2 + 1, :], axis=-1, keepdims=True)
        ck = jnp.concatenate([c[t2]] * n_new, axis=0)
        val = col + cq - ck
        s_new = jnp.where((lane == t2) & (row_t >= t2), val, s_new)

    m = jnp.max(s_new, axis=-1, keepdims=True)
    for sp in s_pages:
        m = jnp.maximum(m, jnp.max(sp, axis=-1, keepdims=True))

    p_new = jnp.exp(s_new - m)
    lsum = jnp.sum(p_new, axis=-1, keepdims=True)
    o = jnp.zeros((n_rows, WIDTH), F32)
    for t2 in range(n_new):
        o = o + p_new[:, t2:t2 + 1] * vn[t2:t2 + 1, :]
    for p in range(n_pages):
        pp = jnp.exp(s_pages[p] - m)
        lsum = lsum + jnp.sum(pp, axis=-1, keepdims=True)
        o = o + _dot_nt(pp.astype(BF16), vbuf[slot, p].astype(BF16))
    o = o / lsum

    outs = []
    for t in range(n_new):
        ot = jnp.where(bd, o[t * N_HEADS:(t + 1) * N_HEADS], 0.0)
        outs.append(jnp.sum(ot, axis=0, keepdims=True))
    o_ref[0] = jnp.concatenate(outs, axis=0)


def _attn_sample(l, page_table, q, kn, vn, lfn_t, ck_t, cv_t, cl_t, consts):
    S, n_new, _ = q.shape
    n_pages = page_table.shape[1]
    per_seq = lambda shape: pl.BlockSpec((1,) + shape, lambda b, pt: (b, 0, 0))
    const = lambda shape: pl.BlockSpec(shape, lambda b, pt: (0, 0), pipeline_mode=pl.Buffered(1))
    n_ph = n_pages * N_HEADS
    grid_spec = pltpu.PrefetchScalarGridSpec(
        num_scalar_prefetch=1,
        grid=(S,),
        in_specs=[
            per_seq((n_new, WIDTH)), per_seq((n_new, WIDTH)), per_seq((n_new, WIDTH)),
            per_seq((N_HEADS, LANES)),
            pl.BlockSpec(memory_space=pl.ANY), pl.BlockSpec(memory_space=pl.ANY),
            pl.BlockSpec(memory_space=pl.ANY),
            const((PAGE, PAGE)), const((PAGE, PAGE)), const((n_ph, n_ph)),
        ],
        out_specs=per_seq((n_new, WIDTH)),
        scratch_shapes=[
            pltpu.VMEM((2, n_pages, WIDTH, PAGE), F32),
            pltpu.VMEM((2, n_pages, WIDTH, PAGE), F32),
            pltpu.VMEM((2, n_pages, N_HEADS, PAGE), F32),
            pltpu.SemaphoreType.DMA((3, 2)),
        ],
    )
    return pl.pallas_call(
        functools.partial(_attn_sample_kernel, l, n_pages),
        grid_spec=grid_spec,
        out_shape=jax.ShapeDtypeStruct((S, n_new, WIDTH), F32),
        compiler_params=pltpu.CompilerParams(
            dimension_semantics=("arbitrary",), vmem_limit_bytes=VMEM_LIMIT),
        name="attn_sample",
    )(page_table, q, kn, vn, lfn_t, ck_t, cv_t, cl_t, consts["sl"], consts["ones"], consts["lsuf"])


def _post_kernel(x_ref, a_ref, sga_ref, mb_ref, mod_ref, n2_ref, waup_ref, wout_ref, w1_ref, w2_ref,
                 o_ref, acc_ref, h2_ref):
    m = mod_ref[0]
    g1, sh2, sc2, g2 = m[2], m[3], m[4], m[5]
    au = _dot(a_ref[0].astype(BF16), waup_ref[0])
    merged = sga_ref[0].astype(F32) * au + mb_ref[0].astype(F32)
    x1 = x_ref[0] + g1 * _dot(merged.astype(BF16), wout_ref[0])
    o_ref[0] = x1
    h2 = _rms(x1) * n2_ref[0]
    h2_ref[...] = (h2 * (1.0 + sc2) + sh2).astype(BF16)
    acc_ref[...] = jnp.zeros_like(acc_ref)

    def ffn_chunk(c, carry):
        gu = _dot(h2_ref[...], w1_ref[0, c])
        gate = gu[:, :FF_CHUNK]
        act = (gate * _sigmoid(gate) * gu[:, FF_CHUNK:]).astype(BF16)
        acc_ref[...] += _dot(act, w2_ref[0, c])
        return carry

    lax.fori_loop(0, N_FF_CHUNKS, ffn_chunk, 0)
    o_ref[0] = o_ref[0] + g2 * acc_ref[...]


def _post(l, x, a, sga, mb, mod, w, mod_rows, mod_layer_indexed):
    B, T, _ = x.shape
    tm = min(TM, T)
    grid = (B, T // tm)
    row = lambda width: pl.BlockSpec((1, tm, width), lambda b, i: (b, i, 0))
    if mod_layer_indexed:
        mod_spec = pl.BlockSpec((1, N_ADA, mod_rows, D_MODEL), lambda b, i: (l, 0, 0, 0),
                                pipeline_mode=pl.Buffered(1))
    else:
        mod_spec = pl.BlockSpec((1, N_ADA, mod_rows, D_MODEL), lambda b, i: (b, 0, 0, 0))
    return pl.pallas_call(
        _post_kernel,
        grid=grid,
        in_specs=[
            row(D_MODEL), row(WIDTH), row(D_MODEL), row(D_MODEL), mod_spec,
            _layer_spec((1, D_MODEL), l, 2),
            _layer_spec((WIDTH, D_MODEL), l, 2),
            _layer_spec((D_MODEL, D_MODEL), l, 2),
            _layer_spec((N_FF_CHUNKS, D_MODEL, 2 * FF_CHUNK), l, 2),
            _layer_spec((N_FF_CHUNKS, FF_CHUNK, D_MODEL), l, 2),
        ],
        out_specs=row(D_MODEL),
        out_shape=jax.ShapeDtypeStruct((B, T, D_MODEL), F32),
        scratch_shapes=[pltpu.VMEM((tm, D_MODEL), F32), pltpu.VMEM((tm, D_MODEL), BF16)],
        compiler_params=pltpu.CompilerParams(
            dimension_semantics=("arbitrary", "arbitrary"), vmem_limit_bytes=VMEM_LIMIT),
        name="post",
    )(x, a, sga, mb, mod, w["n2"], w["waup"], w["wout"], w["w1"], w["w2"])


def _prepare_weights(norm1_g, norm2_g, w_in, b_f, q_norm_g, k_norm_g, v_ln_g, v_ln_b, w_spatial,
                     b_spatial, w_a_up, w_b_up, w_out, w_ffn_in, w_ffn_out, n_new, n_seq):
    c0 = 3 * WIDTH
    c1 = c0 + N_HEADS
    c2 = c1 + 2 * WIDTH
    w = {}
    w["n1"] = norm1_g.reshape(DEPTH, 1, D_MODEL)
    w["n2"] = norm2_g.reshape(DEPTH, 1, D_MODEL)
    w["wqkv"] = w_in[:, :, :c0].astype(BF16)
    wf8 = jnp.pad(w_in[:, :, c0:c1], ((0, 0), (0, 0), (0, 2 * SUBLANES - N_HEADS)))
    w["wf"] = jnp.pad(jnp.tile(wf8, (1, 1, 3)), ((0, 0), (0, 0), (0, LANES - 6 * SUBLANES))).astype(BF16)
    w["wzb"] = w_in[:, :, c1:c2].astype(BF16)
    w["wg"] = w_in[:, :, c2:].astype(BF16)
    bf8 = jnp.pad(b_f, ((0, 0), (0, 2 * SUBLANES - N_HEADS)))
    w["bf"] = jnp.pad(jnp.tile(bf8, (1, 3)), ((0, 0), (0, LANES - 6 * SUBLANES))).reshape(DEPTH, 1, LANES)
    w["qg"] = jnp.tile(q_norm_g, (1, N_HEADS)).reshape(DEPTH, 1, WIDTH)
    w["kg"] = jnp.tile(k_norm_g, (1, N_HEADS)).reshape(DEPTH, 1, WIDTH)
    w["lng"] = v_ln_g.reshape(DEPTH, 1, WIDTH)
    w["lnb"] = v_ln_b.reshape(DEPTH, 1, WIDTH)
    w["waup"] = w_a_up.astype(BF16)
    w["wbup"] = w_b_up.astype(BF16)
    w["wout"] = w_out.astype(BF16)
    gate = w_ffn_in[:, :, :D_FF].reshape(DEPTH, D_MODEL, N_FF_CHUNKS, FF_CHUNK)
    up = w_ffn_in[:, :, D_FF:].reshape(DEPTH, D_MODEL, N_FF_CHUNKS, FF_CHUNK)
    w["w1"] = jnp.concatenate([gate, up], axis=-1).transpose(0, 2, 1, 3).astype(BF16)
    w["w2"] = w_ffn_out.reshape(DEPTH, N_FF_CHUNKS, FF_CHUNK, D_MODEL).astype(BF16)
    wt = jnp.tril(w_spatial)
    wt = wt.reshape(DEPTH, WIDTH // LANES, 2, CHUNK, CHUNK)
    w["wpair"] = jnp.concatenate([wt[:, :, 0], wt[:, :, 1]], axis=-1).astype(BF16)
    w["bsp"] = jnp.repeat(b_spatial.transpose(0, 2, 1), HEAD_DIM, axis=2)
    ws = jnp.tril(w_spatial[:, :, :n_new, :n_new])
    coefs = []
    for d in range(n_new):
        diag = jnp.stack([ws[:, :, t, t - d] if t >= d else jnp.zeros_like(ws[:, :, 0, 0])
                          for t in range(n_new)], axis=1)
        coefs.append(jnp.tile(jnp.repeat(diag, HEAD_DIM, axis=2), (1, n_seq, 1)))
    w["coef"] = jnp.stack(coefs, axis=1)
    bs = jnp.repeat(b_spatial[:, :, :n_new].transpose(0, 2, 1), HEAD_DIM, axis=2)
    w["bsp_s"] = jnp.tile(bs, (1, n_seq, 1))
    return w


def _constants(n_pages):
    r = jnp.arange(WIDTH)
    gmat = jnp.where((r[:, None] // HEAD_DIM) == (r[None, :] // HEAD_DIM), 1.0 / HEAD_DIM, 0.0)
    t = jnp.arange(TM)
    umat = (t[:, None] <= t[None, :])
    p = jnp.arange(PAGE)
    sl = (p[:, None] > p[None, :])
    ph = jnp.arange(n_pages * N_HEADS)
    lsuf = ((ph[:, None] % N_HEADS) == (ph[None, :] % N_HEADS)) & (
        (ph[None, :] // N_HEADS) > (ph[:, None] // N_HEADS))
    return {
        "gmat": gmat.astype(BF16),
        "umat": umat.astype(BF16),
        "ltri": umat.T.astype(BF16),
        "sl": sl.astype(BF16),
        "ones": jnp.ones((PAGE, PAGE), BF16),
        "lsuf": lsuf.astype(BF16),
    }


def kernel(x_prompt, x_sample, cache_k, cache_v, cache_logf, page_table, c_prompt, c_sample, ada_w, ada_b, norm1_g, norm2_g, w_in, b_f, q_norm_g, k_norm_g, v_ln_g, v_ln_b, w_spatial, b_spatial, w_a_up, w_b_up, w_out, w_ffn_in, w_ffn_out):
    B, T, _ = x_prompt.shape
    S, n_new, _ = x_sample.shape
    n_pages = page_table.shape[1]
    R = S * n_new

    w = _prepare_weights(norm1_g, norm2_g, w_in, b_f, q_norm_g, k_norm_g, v_ln_g, v_ln_b, w_spatial,
                         b_spatial, w_a_up, w_b_up, w_out, w_ffn_in, w_ffn_out, n_new, S)
    consts = _constants(n_pages)

    n_c = B + S
    pad = (-n_c) % SUBLANES
    c_all = jnp.concatenate([c_prompt, c_sample, jnp.zeros((pad, D_MODEL), F32)], axis=0)
    mod = _modulation(c_all, ada_w, ada_b)
    mod_p = mod[:, :B].reshape(DEPTH, B, N_ADA, 1, D_MODEL)
    mod_s = jnp.repeat(mod[:, B:n_c].reshape(DEPTH, S, N_ADA, D_MODEL), n_new, axis=1)
    mod_s = mod_s.transpose(0, 2, 1, 3)

    ck_t = cache_k.transpose(0, 1, 3, 4, 2).reshape(DEPTH, -1, WIDTH, PAGE)
    cv_t = cache_v.transpose(0, 1, 3, 4, 2).reshape(DEPTH, -1, WIDTH, PAGE)
    cl_t = cache_logf.transpose(0, 1, 3, 2)

    k_all = jnp.zeros((DEPTH, B, WIDTH, T), F32)
    v_all = jnp.zeros((DEPTH, B, WIDTH, T), F32)
    l_all = jnp.zeros((DEPTH, B, N_HEADS, T), F32)

    yp = x_prompt
    ys = x_sample.reshape(1, R, D_MODEL)
    ks_l, vs_l, ls_l, cv_l = [], [], [], []
    for l in range(DEPTH):
        qt, kb, vtb, ft, fb, sga, mb, k_all, v_all, l_all = _inproj_prompt(
            l, yp, mod_p[l], w, consts, k_all, v_all, l_all)
        jlo = _first_key_block(ft, q_norm_g[l], k_norm_g[l], TQ)
        a = _attn_prompt(qt, kb, fb, vtb, jlo)
        yp = _post(l, yp, a, sga, mb, mod_p[l], w, 1, False)

        qs, kn, vn, lfn, cvs, sga_s, mb_s = _inproj_sample(l, ys, mod_s, w, consts)
        lfn_t = lfn[0, :, :N_HEADS].reshape(S, n_new, N_HEADS).transpose(0, 2, 1)
        lfn_t = jnp.pad(lfn_t, ((0, 0), (0, 0), (0, LANES - n_new)))
        a_s = _attn_sample(l, page_table, qs.reshape(S, n_new, WIDTH), kn.reshape(S, n_new, WIDTH),
                           vn.reshape(S, n_new, WIDTH), lfn_t, ck_t, cv_t, cl_t, consts)
        ys = _post(l, ys, a_s.reshape(1, R, WIDTH), sga_s, mb_s, mod_s, w, R, True)

        ks_l.append(kn.reshape(S, n_new, N_HEADS, HEAD_DIM))
        vs_l.append(vn.reshape(S, n_new, N_HEADS, HEAD_DIM))
        ls_l.append(lfn[0, :, :N_HEADS].reshape(S, n_new, N_HEADS))
        cv_l.append(cvs.reshape(S, n_new, WIDTH))

    k_prompt = k_all.reshape(DEPTH, B, N_HEADS, HEAD_DIM, T).transpose(0, 1, 4, 2, 3)
    v_prompt = v_all.reshape(DEPTH, B, N_HEADS, HEAD_DIM, T).transpose(0, 1, 4, 2, 3)
    logf_prompt = l_all.transpose(0, 1, 3, 2)
    return (yp, ys.reshape(S, n_new, D_MODEL), k_prompt, v_prompt, logf_prompt,
            jnp.stack(ks_l), jnp.stack(vs_l), jnp.stack(ls_l), jnp.stack(cv_l))
```

```python
import functools
import math

import jax
import jax.numpy as jnp
import numpy as np
from jax import lax
from jax.experimental import pallas as pl
from jax.experimental.pallas import tpu as pltpu

F32 = jnp.float32
BF16 = jnp.bfloat16

D_MODEL = 1024
DEPTH = 4
N_HEADS = 8
HEAD_DIM = 64
WIDTH = N_HEADS * HEAD_DIM
CHUNK = 128
PAGE = 128
D_FF = 2816
FF_CHUNK = 256
N_FF_CHUNKS = D_FF // FF_CHUNK
N_ADA = 6
EPS = 1e-6
NEG_INF = -1e30
SCALE = HEAD_DIM ** -0.5
LOG2E = math.log2(math.e)
ZERO_WEIGHT_EXPONENT = -105.0
LANES = 128
SUBLANES = 8
VMEM_LIMIT = 56 * 1024 * 1024

TM = 512
TQ = 256
PAIRS_PER_LOOP = 4


def _dot(a, b):
    return jnp.dot(a, b, preferred_element_type=F32)


def _dot_nt(a, b):
    return lax.dot_general(a, b, (((1,), (1,)), ((), ())), preferred_element_type=F32)


def _split3(a):
    hi = a.astype(BF16)
    r = a - hi.astype(F32)
    mid = r.astype(BF16)
    lo = (r - mid.astype(F32)).astype(BF16)
    return hi, mid, lo


def _dot3_lhs(a, b):
    hi, mid, lo = _split3(a)
    return (_dot(lo, b) + _dot(mid, b)) + _dot(hi, b)


def _dot3_rhs(a, b):
    hi, mid, lo = _split3(b)
    return (_dot(a, lo) + _dot(a, mid)) + _dot(a, hi)


def _sigmoid(x):
    return 1.0 / (1.0 + jnp.exp(-x))


def _log_sigmoid(x):
    return jnp.minimum(x, 0.0) - jnp.log1p(jnp.exp(-jnp.abs(x)))


def _gelu_tanh(x):
    c = math.sqrt(2.0 / math.pi)
    return 0.5 * x * (1.0 + jnp.tanh(c * (x + 0.044715 * (x * x * x))))


def _rms(x):
    return x * lax.rsqrt(jnp.mean(x * x, axis=-1, keepdims=True) + EPS)


def _mod_kernel(c_ref, w_ref, b_ref, o_ref):
    c = c_ref[...]
    s = (c * _sigmoid(c)).astype(BF16)
    o_ref[0, 0] = _dot(s, w_ref[0].astype(BF16)) + b_ref[0]


def _modulation(c_all, ada_w, ada_b):
    rows = c_all.shape[0]
    n_col = N_ADA * D_MODEL
    return pl.pallas_call(
        _mod_kernel,
        grid=(DEPTH, N_ADA),
        in_specs=[
            pl.BlockSpec((rows, D_MODEL), lambda l, n: (0, 0)),
            pl.BlockSpec((1, D_MODEL, D_MODEL), lambda l, n: (l, 0, n)),
            pl.BlockSpec((1, 1, D_MODEL), lambda l, n: (l, 0, n)),
        ],
        out_specs=pl.BlockSpec((1, 1, rows, D_MODEL), lambda l, n: (l, n, 0, 0)),
        out_shape=jax.ShapeDtypeStruct((DEPTH, N_ADA, rows, D_MODEL), F32),
        compiler_params=pltpu.CompilerParams(vmem_limit_bytes=VMEM_LIMIT),
        name="adaln_modulation",
    )(c_all, ada_w, ada_b.reshape(DEPTH, 1, n_col))


def _inproj_common(x, m, n1_ref, wqkv_ref, wf_ref, wzb_ref, wg_ref, bf_ref, qg_ref, kg_ref,
                   lng_ref, lnb_ref, gmat_ref, q_scale):
    sh1, sc1 = m[0], m[1]
    h = _rms(x) * n1_ref[0]
    h = h * (1.0 + sc1) + sh1
    hb = h.astype(BF16)

    qkv = _dot(hb, wqkv_ref[0])
    gmat = gmat_ref[...]

    def headnorm(z, g):
        sq = z * z
        hi = sq.astype(BF16)
        lo = (sq - hi.astype(F32)).astype(BF16)
        ms = _dot(hi, gmat) + _dot(lo, gmat)
        return z * lax.rsqrt(ms + EPS) * g

    q = headnorm(qkv[:, :WIDTH], qg_ref[0]) * q_scale
    k = headnorm(qkv[:, WIDTH:2 * WIDTH], kg_ref[0])
    v = qkv[:, 2 * WIDTH:]

    logf = _log_sigmoid(_dot(hb, wf_ref[0]) + bf_ref[0])

    zb = _gelu_tanh(_dot(hb, wzb_ref[0]))
    u = zb[:, :WIDTH]
    vb = zb[:, WIDTH:]
    mu = jnp.mean(vb, axis=-1, keepdims=True)
    xc = vb - mu
    vbn = xc * lax.rsqrt(jnp.mean(xc * xc, axis=-1, keepdims=True) + EPS) * lng_ref[0] + lnb_ref[0]

    g = _dot(hb, wg_ref[0])
    sga = _sigmoid(g[:, :D_MODEL])
    sgb = _sigmoid(g[:, D_MODEL:])
    return q, k, v, logf, u, vbn, sga, sgb


def _inproj_prompt_kernel(x_ref, mod_ref, n1_ref, wqkv_ref, wf_ref, wzb_ref, wg_ref, wbup_ref, bf_ref,
                          qg_ref, kg_ref, lng_ref, lnb_ref, gmat_ref, wpair_ref, bsp_ref, umat_ref,
                          ltri_ref, kprev_ref, vprev_ref, lprev_ref,
                          qt_ref, kb_ref, vtb_ref, ft_ref, fb_ref, sga_ref, mb_ref, kt_ref, vt_ref, lt_ref,
                          carry_ref, carry_row_ref):
    del kprev_ref, vprev_ref, lprev_ref
    tm = x_ref.shape[1]
    q, k, v, logf, u, vbn, sga, sgb = _inproj_common(
        x_ref[0], mod_ref[0], n1_ref, wqkv_ref, wf_ref, wzb_ref, wg_ref, bf_ref, qg_ref, kg_ref,
        lng_ref, lnb_ref, gmat_ref, SCALE * LOG2E)

    qt_ref[0] = q.T.astype(BF16)
    kt_ref[0, 0] = k.T
    kb_ref[0] = k.astype(BF16)
    vt = v.T
    vt_ref[0, 0] = vt
    vtb_ref[0] = vt.astype(BF16)

    lt = logf.T[:N_HEADS]
    lt_ref[0, 0] = lt

    @pl.when(pl.program_id(1) == 0)
    def _():
        carry_ref[...] = jnp.zeros_like(carry_ref)
        carry_row_ref[...] = jnp.zeros_like(carry_row_ref)

    fcum = _dot3_lhs(lt, umat_ref[...]) + carry_ref[:, 0:1]
    ft_ref[0] = fcum
    carry_ref[...] = jnp.broadcast_to(fcum[:, tm - 1:tm], carry_ref.shape)

    frow = _dot3_rhs(ltri_ref[...], logf) + carry_row_ref[0:1, :]
    carry_row_ref[...] = jnp.broadcast_to(frow[tm - 1:tm, :], carry_row_ref.shape)
    hi, mid, lo = _split3(frow * (-LOG2E))
    lane_id = lax.broadcasted_iota(jnp.int32, (1, LANES), 1)
    piece = jnp.where(lane_id < 2 * SUBLANES, hi.astype(F32),
                      jnp.where(lane_id < 4 * SUBLANES, mid.astype(F32), lo.astype(F32)))
    used = (lane_id % (2 * SUBLANES) < N_HEADS) & (lane_id < 6 * SUBLANES)
    fb_ref[0] = jnp.where(used, piece, 0.0).astype(BF16)

    lane = lax.broadcasted_iota(jnp.int32, (1, LANES), 1)
    left = lane < HEAD_DIM
    chunks = []
    for c in range(tm // CHUNK):
        xc = vbn[c * CHUNK:(c + 1) * CHUNK]
        cols = []
        for jp in range(WIDTH // LANES):
            xp = xc[:, jp * LANES:(jp + 1) * LANES]
            xs = jnp.concatenate([jnp.where(left, xp, 0.0), jnp.where(left, 0.0, xp)], axis=0)
            cols.append(_dot(wpair_ref[0, jp], xs.astype(BF16)))
        chunks.append(jnp.concatenate(cols, axis=1) + bsp_ref[0])
    s = jnp.concatenate(chunks, axis=0)
    bg = (u * s).astype(BF16)

    sga_ref[0] = sga.astype(BF16)
    mb_ref[0] = (sgb * _dot(bg, wbup_ref[0])).astype(BF16)


def _inproj_sample_kernel(x_ref, mod_ref, n1_ref, wqkv_ref, wf_ref, wzb_ref, wg_ref, wbup_ref, bf_ref,
                          qg_ref, kg_ref, lng_ref, lnb_ref, gmat_ref, coef_ref, bsp_ref,
                          q_ref, k_ref, v_ref, lf_ref, cv_ref, sga_ref, mb_ref):
    q, k, v, logf, u, vbn, sga, sgb = _inproj_common(
        x_ref[0], mod_ref[0], n1_ref, wqkv_ref, wf_ref, wzb_ref, wg_ref, bf_ref, qg_ref, kg_ref,
        lng_ref, lnb_ref, gmat_ref, SCALE)
    q_ref[0] = q
    k_ref[0] = k
    v_ref[0] = v
    lf_ref[0] = logf
    cv_ref[0] = vbn

    n_rows = vbn.shape[0]
    groups = lambda a: a.reshape(n_rows // SUBLANES, SUBLANES, WIDTH)
    s = bsp_ref[0][None] + coef_ref[0, 0][None] * groups(vbn)
    for d in range(1, coef_ref.shape[1]):
        s = s + coef_ref[0, d][None] * groups(pltpu.roll(vbn, d, axis=0))
    bg = (u * s.reshape(n_rows, WIDTH)).astype(BF16)
    sga_ref[0] = sga.astype(BF16)
    mb_ref[0] = (sgb * _dot(bg, wbup_ref[0])).astype(BF16)


def _layer_spec(tail, l, nidx):
    zeros = (0,) * len(tail)
    if nidx == 1:
        imap = lambda i: (l,) + zeros
    else:
        imap = lambda b, i: (l,) + zeros
    return pl.BlockSpec((1,) + tuple(tail), imap, pipeline_mode=pl.Buffered(1))


def _const_spec(shape, nidx):
    zeros = (0,) * len(shape)
    if nidx == 1:
        imap = lambda i: zeros
    else:
        imap = lambda b, i: zeros
    return pl.BlockSpec(tuple(shape), imap, pipeline_mode=pl.Buffered(1))


def _inproj_prompt(l, x, mod, w, consts, k_all, v_all, l_all):
    B, T, _ = x.shape
    tm = TM
    grid = (B, T // tm)
    row = lambda width: pl.BlockSpec((1, tm, width), lambda b, i: (b, i, 0))
    col = lambda height: pl.BlockSpec((1, height, tm), lambda b, i: (b, 0, i))
    stacked = lambda height: pl.BlockSpec((1, 1, height, tm), lambda b, i: (l, b, 0, i))
    in_specs = [
        row(D_MODEL),
        pl.BlockSpec((1, N_ADA, 1, D_MODEL), lambda b, i: (b, 0, 0, 0)),
        _layer_spec((1, D_MODEL), l, 2),
        _layer_spec((D_MODEL, 3 * WIDTH), l, 2),
        _layer_spec((D_MODEL, LANES), l, 2),
        _layer_spec((D_MODEL, 2 * WIDTH), l, 2),
        _layer_spec((D_MODEL, 2 * D_MODEL), l, 2),
        _layer_spec((WIDTH, D_MODEL), l, 2),
        _layer_spec((1, LANES), l, 2),
        _layer_spec((1, WIDTH), l, 2),
        _layer_spec((1, WIDTH), l, 2),
        _layer_spec((1, WIDTH), l, 2),
        _layer_spec((1, WIDTH), l, 2),
        _const_spec((WIDTH, WIDTH), 2),
        _layer_spec((WIDTH // LANES, CHUNK, 2 * CHUNK), l, 2),
        _layer_spec((CHUNK, WIDTH), l, 2),
        _const_spec((tm, tm), 2),
        _const_spec((tm, tm), 2),
        pl.BlockSpec(memory_space=pl.ANY),
        pl.BlockSpec(memory_space=pl.ANY),
        pl.BlockSpec(memory_space=pl.ANY),
    ]
    out_specs = [col(WIDTH), row(WIDTH), col(WIDTH), col(N_HEADS), row(LANES), row(D_MODEL),
                 row(D_MODEL), stacked(WIDTH), stacked(WIDTH), stacked(N_HEADS)]
    out_shape = [
        jax.ShapeDtypeStruct((B, WIDTH, T), BF16),
        jax.ShapeDtypeStruct((B, T, WIDTH), BF16),
        jax.ShapeDtypeStruct((B, WIDTH, T), BF16),
        jax.ShapeDtypeStruct((B, N_HEADS, T), F32),
        jax.ShapeDtypeStruct((B, T, LANES), BF16),
        jax.ShapeDtypeStruct((B, T, D_MODEL), BF16),
        jax.ShapeDtypeStruct((B, T, D_MODEL), BF16),
        jax.ShapeDtypeStruct(k_all.shape, F32),
        jax.ShapeDtypeStruct(v_all.shape, F32),
        jax.ShapeDtypeStruct(l_all.shape, F32),
    ]
    n_in = len(in_specs)
    return pl.pallas_call(
        _inproj_prompt_kernel,
        grid=grid,
        in_specs=in_specs,
        out_specs=out_specs,
        out_shape=out_shape,
        scratch_shapes=[pltpu.VMEM((N_HEADS, LANES), F32), pltpu.VMEM((SUBLANES, LANES), F32)],
        input_output_aliases={n_in - 3: 7, n_in - 2: 8, n_in - 1: 9},
        compiler_params=pltpu.CompilerParams(
            dimension_semantics=("arbitrary", "arbitrary"), vmem_limit_bytes=VMEM_LIMIT),
        name="inproj_prompt",
    )(x, mod, w["n1"], w["wqkv"], w["wf"], w["wzb"], w["wg"], w["wbup"], w["bf"], w["qg"], w["kg"],
      w["lng"], w["lnb"], consts["gmat"], w["wpair"], w["bsp"], consts["umat"], consts["ltri"],
      k_all, v_all, l_all)


def _inproj_sample(l, x, mod, w, consts):
    _, R, _ = x.shape
    full = lambda width: pl.BlockSpec((1, R, width), lambda i: (0, 0, 0))
    in_specs = [
        full(D_MODEL),
        pl.BlockSpec((1, N_ADA, R, D_MODEL), lambda i: (l, 0, 0, 0), pipeline_mode=pl.Buffered(1)),
        _layer_spec((1, D_MODEL), l, 1),
        _layer_spec((D_MODEL, 3 * WIDTH), l, 1),
        _layer_spec((D_MODEL, LANES), l, 1),
        _layer_spec((D_MODEL, 2 * WIDTH), l, 1),
        _layer_spec((D_MODEL, 2 * D_MODEL), l, 1),
        _layer_spec((WIDTH, D_MODEL), l, 1),
        _layer_spec((1, LANES), l, 1),
        _layer_spec((1, WIDTH), l, 1),
        _layer_spec((1, WIDTH), l, 1),
        _layer_spec((1, WIDTH), l, 1),
        _layer_spec((1, WIDTH), l, 1),
        _const_spec((WIDTH, WIDTH), 1),
        _layer_spec(w["coef"].shape[1:], l, 1),
        _layer_spec((SUBLANES, WIDTH), l, 1),
    ]
    out_specs = [full(WIDTH), full(WIDTH), full(WIDTH), full(LANES), full(WIDTH), full(D_MODEL),
                 full(D_MODEL)]
    out_shape = [
        jax.ShapeDtypeStruct((1, R, WIDTH), F32),
        jax.ShapeDtypeStruct((1, R, WIDTH), F32),
        jax.ShapeDtypeStruct((1, R, WIDTH), F32),
        jax.ShapeDtypeStruct((1, R, LANES), F32),
        jax.ShapeDtypeStruct((1, R, WIDTH), F32),
        jax.ShapeDtypeStruct((1, R, D_MODEL), BF16),
        jax.ShapeDtypeStruct((1, R, D_MODEL), BF16),
    ]
    return pl.pallas_call(
        _inproj_sample_kernel,
        grid=(1,),
        in_specs=in_specs,
        out_specs=out_specs,
        out_shape=out_shape,
        compiler_params=pltpu.CompilerParams(vmem_limit_bytes=VMEM_LIMIT),
        name="inproj_sample",
    )(x, mod, w["n1"], w["wqkv"], w["wf"], w["wzb"], w["wg"], w["wbup"], w["bf"], w["qg"], w["kg"],
      w["lng"], w["lnb"], consts["gmat"], w["coef"], w["bsp_s"])


def _attn_prompt_kernel(jlo_ref, qt_ref, k_ref, fb_ref, vt_ref, o_ref, qa_sc, m_sc, acc_sc):
    b = pl.program_id(0)
    i = pl.program_id(1)
    tq = qt_ref.shape[2]
    sub = lax.broadcasted_iota(jnp.int32, (LANES, tq), 0)
    key_ix = lax.broadcasted_iota(jnp.int32, (tq, tq), 0)
    qry_ix = lax.broadcasted_iota(jnp.int32, (tq, tq), 1)

    keeps, ones_rows = [], []
    for half in range(2):
        mine = (sub // HEAD_DIM) == half
        keeps.append(jnp.where(mine, 1.0, 0.0).astype(BF16))
        l_row = HEAD_DIM if half == 0 else 0
        ones_rows.append(jnp.where(sub == l_row, 1.0, 0.0).astype(BF16))

    n_pairs = WIDTH // LANES
    for g in range(n_pairs // PAIRS_PER_LOOP):
        pairs = range(g * PAIRS_PER_LOOP, (g + 1) * PAIRS_PER_LOOP)
        jlo = i
        for jp in pairs:
            qtp = qt_ref[0, jp * LANES:(jp + 1) * LANES, :].astype(F32)
            for half in range(2):
                h = 2 * jp + half
                slot = h - 2 * pairs[0]
                mine = (sub // HEAD_DIM) == half
                sel = (sub == h) | (sub == 2 * SUBLANES + h) | (sub == 4 * SUBLANES + h)
                qa_sc[slot] = jnp.concatenate(
                    [jnp.where(mine, qtp, 0.0), jnp.where(sel, 1.0, 0.0)], axis=0).astype(BF16)
                m_sc[slot] = jnp.full((1, tq), NEG_INF, F32)
                acc_sc[slot] = jnp.zeros((LANES, tq), F32)
                jlo = jnp.minimum(jlo, jlo_ref[b, i, h])

        def scores_of(j, pairs=pairs):
            off = pl.multiple_of(j * tq, tq)
            fbj = fb_ref[0, pl.ds(off, tq), :]
            scores = []
            for jp in pairs:
                ka = jnp.concatenate(
                    [k_ref[0, pl.ds(off, tq), jp * LANES:(jp + 1) * LANES], fbj], axis=1)
                for half in range(2):
                    scores.append(_dot(ka, qa_sc[2 * (jp - pairs[0]) + half]))
            return tuple(scores)

        def softmax_pv(j, scores, diagonal, pairs=pairs):
            off = pl.multiple_of(j * tq, tq)
            probs, alphas = [], []
            for slot, s in enumerate(scores):
                if diagonal:
                    s = jnp.where(key_ix <= qry_ix, s, NEG_INF)
                m_prev = m_sc[slot]
                m_new = jnp.maximum(m_prev, jnp.max(s, axis=0, keepdims=True))
                alphas.append(jnp.exp2(m_prev - m_new))
                probs.append(jnp.exp2(s - m_new).astype(BF16))
                m_sc[slot] = m_new
            for jp in pairs:
                vt = vt_ref[0, jp * LANES:(jp + 1) * LANES, pl.ds(off, tq)]
                for half in range(2):
                    slot = 2 * (jp - pairs[0]) + half
                    va = vt * keeps[half] + ones_rows[half]
                    acc_sc[slot] = alphas[slot] * acc_sc[slot] + _dot(va, probs[slot])

        n_off = i - jlo
        odd = lax.rem(n_off, 2)

        @pl.when(odd == 1)
        def _():
            softmax_pv(jlo, scores_of(jlo), False)

        def two_blocks(t, carry, start=jlo + odd):
            j = start + 2 * t
            s0 = scores_of(j)
            s1 = scores_of(j + 1)
            softmax_pv(j, s0, False)
            softmax_pv(j + 1, s1, False)
            return carry

        lax.fori_loop(0, n_off // 2, two_blocks, 0)
        softmax_pv(i, scores_of(i), True)

        for jp in pairs:
            a0 = acc_sc[2 * (jp - pairs[0])]
            a1 = acc_sc[2 * (jp - pairs[0]) + 1]
            out_t = jnp.concatenate(
                [a0[:HEAD_DIM] * (1.0 / a0[HEAD_DIM:HEAD_DIM + 1]), a1[HEAD_DIM:] * (1.0 / a1[0:1])],
                axis=0)
            o_ref[0, :, jp * LANES:(jp + 1) * LANES] = out_t.T.astype(o_ref.dtype)


def _first_key_block(ft, qg, kg, tq):
    B, H, T = ft.shape
    n = T // tq
    qk_bound = HEAD_DIM * SCALE * jnp.max(jnp.abs(qg)) * jnp.max(jnp.abs(kg))
    f_first = ft[:, :, 0::tq]
    f_last = ft[:, :, tq - 1::tq]
    exponent = 2.0 * qk_bound + f_first[:, :, :, None] - f_last[:, :, None, :]
    earlier = jnp.arange(n)[None, :] < jnp.arange(n)[:, None]
    dead = (exponent < ZERO_WEIGHT_EXPONENT) & earlier[None, None]
    jlo = jnp.min(jnp.where(dead, n, jnp.arange(n, dtype=jnp.int32)), axis=-1)
    return jlo.transpose(0, 2, 1).astype(jnp.int32)


def _attn_prompt(qt, kb, fb, vtb, jlo):
    B, T, _ = kb.shape
    tq = TQ
    resident = lambda shape: pl.BlockSpec(shape, lambda b, i, jl: (b, 0, 0), pipeline_mode=pl.Buffered(1))
    grid_spec = pltpu.PrefetchScalarGridSpec(
        num_scalar_prefetch=1,
        grid=(B, T // tq),
        in_specs=[
            pl.BlockSpec((1, WIDTH, tq), lambda b, i, jl: (b, 0, i)),
            resident((1, T, WIDTH)),
            resident((1, T, LANES)),
            resident((1, WIDTH, T)),
        ],
        out_specs=pl.BlockSpec((1, tq, WIDTH), lambda b, i, jl: (b, i, 0)),
        scratch_shapes=[
            pltpu.VMEM((2 * PAIRS_PER_LOOP, 2 * LANES, tq), BF16),
            pltpu.VMEM((2 * PAIRS_PER_LOOP, 1, tq), F32),
            pltpu.VMEM((2 * PAIRS_PER_LOOP, LANES, tq), F32),
        ],
    )
    return pl.pallas_call(
        _attn_prompt_kernel,
        grid_spec=grid_spec,
        out_shape=jax.ShapeDtypeStruct((B, T, WIDTH), BF16),
        compiler_params=pltpu.CompilerParams(
            dimension_semantics=("arbitrary", "arbitrary"), vmem_limit_bytes=VMEM_LIMIT),
        name="attn_prompt",
    )(jlo, qt, kb, fb, vtb)


def _attn_sample_kernel(l, n_pages, pt_ref, q_ref, kn_ref, vn_ref, lfn_ref, ck_ref, cv_ref, cl_ref,
                        sl_ref, ones_ref, lsuf_ref, o_ref, kbuf, vbuf, lbuf, sem):
    b = pl.program_id(0)
    nb = pl.num_programs(0)
    slot = lax.rem(b, 2)
    n_new = q_ref.shape[1]

    def page_copies(seq, sl):
        cps = []
        for p in range(n_pages):
            page = pt_ref[seq, p]
            cps.append(pltpu.make_async_copy(ck_ref.at[l, page], kbuf.at[sl, p], sem.at[0, sl]))
            cps.append(pltpu.make_async_copy(cv_ref.at[l, page], vbuf.at[sl, p], sem.at[1, sl]))
            cps.append(pltpu.make_async_copy(cl_ref.at[l, page], lbuf.at[sl, p], sem.at[2, sl]))
        return cps

    @pl.when(b == 0)
    def _():
        for cp in page_copies(0, 0):
            cp.start()

    @pl.when(b + 1 < nb)
    def _():
        for cp in page_copies(b + 1, 1 - slot):
            cp.start()

    for cp in page_copies(b, slot):
        cp.wait()

    bd = (lax.broadcasted_iota(jnp.int32, (N_HEADS, WIDTH), 0)
          == lax.broadcasted_iota(jnp.int32, (N_HEADS, WIDTH), 1) // HEAD_DIM)

    q4 = q_ref[0]
    qe = jnp.concatenate(
        [jnp.where(bd, jnp.broadcast_to(q4[t:t + 1, :], (N_HEADS, WIDTH)), 0.0) for t in range(n_new)],
        axis=0)
    qeb = qe.astype(BF16)
    n_rows = n_new * N_HEADS

    lfn = lfn_ref[0]
    c = [lfn[:, 0:1]]
    for t in range(1, n_new):
        c.append(c[-1] + lfn[:, t:t + 1])
    cq = jnp.concatenate(c, axis=0)

    lf = lbuf[slot].reshape(n_pages * N_HEADS, PAGE)
    within = _dot3_lhs(lf, sl_ref[...])
    totals = _dot3_lhs(lf, ones_ref[...])
    decay = within + _dot3_rhs(lsuf_ref[...], totals)

    s_pages = []
    for p in range(n_pages):
        sp = _dot(qeb, kbuf[slot, p].astype(BF16))
        dp = decay[p * N_HEADS:(p + 1) * N_HEADS]
        bias = jnp.concatenate([c[t] + dp for t in range(n_new)], axis=0)
        s_pages.append(sp + bias)

    lane = lax.broadcasted_iota(jnp.int32, (n_rows, LANES), 1)
    row_t = lax.broadcasted_iota(jnp.int32, (n_rows, LANES), 0) // N_HEADS
    kn = kn_ref[0]
    vn = vn_ref[0]
    s_new = jnp.full((n_rows, LANES), NEG_INF, F32)
    for t2 in range(n_new):
        col = jnp.sum(qe * kn[t2:t2 + 1, :], axis=-1, keepdims=True)
        ck = jnp.concatenate([c[t2]] * n_new, axis=0)
        val = col + cq - ck
        s_new = jnp.where((lane == t2) & (row_t >= t2), val, s_new)

    m = jnp.max(s_new, axis=-1, keepdims=True)
    for sp in s_pages:
        m = jnp.maximum(m, jnp.max(sp, axis=-1, keepdims=True))

    p_new = jnp.exp(s_new - m)
    lsum = jnp.sum(p_new, axis=-1, keepdims=True)
    o = jnp.zeros((n_rows, WIDTH), F32)
    for t2 in range(n_new):
        o = o + p_new[:, t2:t2 + 1] * vn[t2:t2 + 1, :]
    for p in range(n_pages):
        pp = jnp.exp(s_pages[p] - m)
        lsum = lsum + jnp.sum(pp, axis=-1, keepdims=True)
        o = o + _dot_nt(pp.astype(BF16), vbuf[slot, p].astype(BF16))
    o = o / lsum

    outs = []
    for t in range(n_new):
        ot = jnp.where(bd, o[t * N_HEADS:(t + 1) * N_HEADS], 0.0)
        outs.append(jnp.sum(ot, axis=0, keepdims=True))
    o_ref[0] = jnp.concatenate(outs, axis=0)


def _attn_sample(l, page_table, q, kn, vn, lfn_t, ck_t, cv_t, cl_t, consts):
    S, n_new, _ = q.shape
    n_pages = page_table.shape[1]
    per_seq = lambda shape: pl.BlockSpec((1,) + shape, lambda b, pt: (b, 0, 0))
    const = lambda shape: pl.BlockSpec(shape, lambda b, pt: (0, 0), pipeline_mode=pl.Buffered(1))
    n_ph = n_pages * N_HEADS
    grid_spec = pltpu.PrefetchScalarGridSpec(
        num_scalar_prefetch=1,
        grid=(S,),
        in_specs=[
            per_seq((n_new, WIDTH)), per_seq((n_new, WIDTH)), per_seq((n_new, WIDTH)),
            per_seq((N_HEADS, LANES)),
            pl.BlockSpec(memory_space=pl.ANY), pl.BlockSpec(memory_space=pl.ANY),
            pl.BlockSpec(memory_space=pl.ANY),
            const((PAGE, PAGE)), const((PAGE, PAGE)), const((n_ph, n_ph)),
        ],
        out_specs=per_seq((n_new, WIDTH)),
        scratch_shapes=[
            pltpu.VMEM((2, n_pages, WIDTH, PAGE), F32),
            pltpu.VMEM((2, n_pages, WIDTH, PAGE), F32),
            pltpu.VMEM((2, n_pages, N_HEADS, PAGE), F32),
            pltpu.SemaphoreType.DMA((3, 2)),
        ],
    )
    return pl.pallas_call(
        functools.partial(_attn_sample_kernel, l, n_pages),
        grid_spec=grid_spec,
        out_shape=jax.ShapeDtypeStruct((S, n_new, WIDTH), F32),
        compiler_params=pltpu.CompilerParams(
            dimension_semantics=("arbitrary",), vmem_limit_bytes=VMEM_LIMIT),
        name="attn_sample",
    )(page_table, q, kn, vn, lfn_t, ck_t, cv_t, cl_t, consts["sl"], consts["ones"], consts["lsuf"])


def _post_kernel(x_ref, a_ref, sga_ref, mb_ref, mod_ref, n2_ref, waup_ref, wout_ref, w1_ref, w2_ref,
                 o_ref, acc_ref, h2_ref):
    m = mod_ref[0]
    g1, sh2, sc2, g2 = m[2], m[3], m[4], m[5]
    au = _dot(a_ref[0].astype(BF16), waup_ref[0])
    merged = sga_ref[0].astype(F32) * au + mb_ref[0].astype(F32)
    x1 = x_ref[0] + g1 * _dot(merged.astype(BF16), wout_ref[0])
    o_ref[0] = x1
    h2 = _rms(x1) * n2_ref[0]
    h2_ref[...] = (h2 * (1.0 + sc2) + sh2).astype(BF16)
    acc_ref[...] = jnp.zeros_like(acc_ref)

    def ffn_chunk(c, carry):
        col = pl.multiple_of(c * FF_CHUNK, FF_CHUNK)
        h2 = h2_ref[...]
        gate = _dot(h2, w1_ref[0, :, pl.ds(col, FF_CHUNK)])
        up = _dot(h2, w1_ref[0, :, pl.ds(pl.multiple_of(D_FF + col, FF_CHUNK), FF_CHUNK)])
        act = (gate * _sigmoid(gate) * up).astype(BF16)
        acc_ref[...] += _dot(act, w2_ref[0, c])
        return carry

    lax.fori_loop(0, N_FF_CHUNKS, ffn_chunk, 0)
    o_ref[0] = o_ref[0] + g2 * acc_ref[...]


def _post(l, x, a, sga, mb, mod, w, mod_rows, mod_layer_indexed):
    B, T, _ = x.shape
    tm = min(TM, T)
    grid = (B, T // tm)
    row = lambda width: pl.BlockSpec((1, tm, width), lambda b, i: (b, i, 0))
    if mod_layer_indexed:
        mod_spec = pl.BlockSpec((1, N_ADA, mod_rows, D_MODEL), lambda b, i: (l, 0, 0, 0),
                                pipeline_mode=pl.Buffered(1))
    else:
        mod_spec = pl.BlockSpec((1, N_ADA, mod_rows, D_MODEL), lambda b, i: (b, 0, 0, 0))
    return pl.pallas_call(
        _post_kernel,
        grid=grid,
        in_specs=[
            row(D_MODEL), row(WIDTH), row(D_MODEL), row(D_MODEL), mod_spec,
            _layer_spec((1, D_MODEL), l, 2),
            _layer_spec((WIDTH, D_MODEL), l, 2),
            _layer_spec((D_MODEL, D_MODEL), l, 2),
            _layer_spec((D_MODEL, 2 * D_FF), l, 2),
            _layer_spec((N_FF_CHUNKS, FF_CHUNK, D_MODEL), l, 2),
        ],
        out_specs=row(D_MODEL),
        out_shape=jax.ShapeDtypeStruct((B, T, D_MODEL), F32),
        scratch_shapes=[pltpu.VMEM((tm, D_MODEL), F32), pltpu.VMEM((tm, D_MODEL), BF16)],
        compiler_params=pltpu.CompilerParams(
            dimension_semantics=("arbitrary", "arbitrary"), vmem_limit_bytes=VMEM_LIMIT),
        name="post",
    )(x, a, sga, mb, mod, w["n2"], w["waup"], w["wout"], w["w1"], w["w2"])


def _prepare_weights(norm1_g, norm2_g, w_in, b_f, q_norm_g, k_norm_g, v_ln_g, v_ln_b, w_spatial,
                     b_spatial, w_a_up, w_b_up, w_out, w_ffn_in, w_ffn_out, n_new):
    c0 = 3 * WIDTH
    c1 = c0 + N_HEADS
    c2 = c1 + 2 * WIDTH
    w = {}
    w["n1"] = norm1_g.reshape(DEPTH, 1, D_MODEL)
    w["n2"] = norm2_g.reshape(DEPTH, 1, D_MODEL)
    w_in = w_in.astype(BF16)
    w["wqkv"] = w_in[:, :, :c0]
    wf8 = jnp.pad(w_in[:, :, c0:c1], ((0, 0), (0, 0), (0, 2 * SUBLANES - N_HEADS)))
    w["wf"] = jnp.pad(jnp.tile(wf8, (1, 1, 3)), ((0, 0), (0, 0), (0, LANES - 6 * SUBLANES)))
    w["wzb"] = w_in[:, :, c1:c2]
    w["wg"] = w_in[:, :, c2:]
    bf8 = jnp.pad(b_f, ((0, 0), (0, 2 * SUBLANES - N_HEADS)))
    w["bf"] = jnp.pad(jnp.tile(bf8, (1, 3)), ((0, 0), (0, LANES - 6 * SUBLANES))).reshape(DEPTH, 1, LANES)
    w["qg"] = jnp.tile(q_norm_g, (1, N_HEADS)).reshape(DEPTH, 1, WIDTH)
    w["kg"] = jnp.tile(k_norm_g, (1, N_HEADS)).reshape(DEPTH, 1, WIDTH)
    w["lng"] = v_ln_g.reshape(DEPTH, 1, WIDTH)
    w["lnb"] = v_ln_b.reshape(DEPTH, 1, WIDTH)
    w["waup"] = w_a_up.astype(BF16)
    w["wbup"] = w_b_up.astype(BF16)
    w["wout"] = w_out.astype(BF16)
    w["w1"] = w_ffn_in.astype(BF16)
    w["w2"] = w_ffn_out.reshape(DEPTH, N_FF_CHUNKS, FF_CHUNK, D_MODEL).astype(BF16)
    wt = jnp.tril(w_spatial)
    wt = wt.reshape(DEPTH, WIDTH // LANES, 2, CHUNK, CHUNK)
    w["wpair"] = jnp.concatenate([wt[:, :, 0], wt[:, :, 1]], axis=-1).astype(BF16)
    w["bsp"] = jnp.repeat(b_spatial.transpose(0, 2, 1), HEAD_DIM, axis=2)
    assert SUBLANES % n_new == 0
    ws = jnp.tril(w_spatial[:, :, :n_new, :n_new])
    coefs = []
    for d in range(n_new):
        diag = jnp.stack([ws[:, :, t, t - d] if t >= d else jnp.zeros_like(ws[:, :, 0, 0])
                          for t in range(n_new)], axis=1)
        coefs.append(jnp.tile(jnp.repeat(diag, HEAD_DIM, axis=2), (1, SUBLANES // n_new, 1)))
    w["coef"] = jnp.stack(coefs, axis=1)
    bs = jnp.repeat(b_spatial[:, :, :n_new].transpose(0, 2, 1), HEAD_DIM, axis=2)
    w["bsp_s"] = jnp.tile(bs, (1, SUBLANES // n_new, 1))
    return w


def _constants(n_pages):
    r = np.arange(WIDTH)
    gmat = np.where((r[:, None] // HEAD_DIM) == (r[None, :] // HEAD_DIM), 1.0 / HEAD_DIM, 0.0)
    t = np.arange(TM)
    umat = (t[:, None] <= t[None, :])
    p = np.arange(PAGE)
    sl = (p[:, None] > p[None, :])
    ph = np.arange(n_pages * N_HEADS)
    lsuf = ((ph[:, None] % N_HEADS) == (ph[None, :] % N_HEADS)) & (
        (ph[None, :] // N_HEADS) > (ph[:, None] // N_HEADS))
    as_bf16 = lambda a: jnp.asarray(np.asarray(a, np.float32), BF16)
    return {
        "gmat": as_bf16(gmat),
        "umat": as_bf16(umat),
        "ltri": as_bf16(umat.T),
        "sl": as_bf16(sl),
        "ones": as_bf16(np.ones((PAGE, PAGE))),
        "lsuf": as_bf16(lsuf),
    }


def kernel(x_prompt, x_sample, cache_k, cache_v, cache_logf, page_table, c_prompt, c_sample, ada_w, ada_b, norm1_g, norm2_g, w_in, b_f, q_norm_g, k_norm_g, v_ln_g, v_ln_b, w_spatial, b_spatial, w_a_up, w_b_up, w_out, w_ffn_in, w_ffn_out):
    B, T, _ = x_prompt.shape
    S, n_new, _ = x_sample.shape
    n_pages = page_table.shape[1]
    R = S * n_new

    w = _prepare_weights(norm1_g, norm2_g, w_in, b_f, q_norm_g, k_norm_g, v_ln_g, v_ln_b, w_spatial,
                         b_spatial, w_a_up, w_b_up, w_out, w_ffn_in, w_ffn_out, n_new)
    consts = _constants(n_pages)

    pad = (-(R + B)) % SUBLANES
    c_all = jnp.concatenate(
        [jnp.repeat(c_sample, n_new, axis=0), c_prompt, jnp.zeros((pad, D_MODEL), F32)], axis=0)
    mod_s = _modulation(c_all, ada_w, ada_b)
    mod_p = mod_s[:, :, R:R + B].transpose(0, 2, 1, 3).reshape(DEPTH, B, N_ADA, 1, D_MODEL)

    ck_t = cache_k.transpose(0, 1, 3, 4, 2).reshape(DEPTH, -1, WIDTH, PAGE)
    cv_t = cache_v.transpose(0, 1, 3, 4, 2).reshape(DEPTH, -1, WIDTH, PAGE)
    cl_t = cache_logf.transpose(0, 1, 3, 2)

    k_all = jnp.zeros((DEPTH, B, WIDTH, T), F32)
    v_all = jnp.zeros((DEPTH, B, WIDTH, T), F32)
    l_all = jnp.zeros((DEPTH, B, N_HEADS, T), F32)

    yp = x_prompt
    ys = x_sample.reshape(1, R, D_MODEL)
    ks_l, vs_l, ls_l, cv_l = [], [], [], []
    for l in range(DEPTH):
        qt, kb, vtb, ft, fb, sga, mb, k_all, v_all, l_all = _inproj_prompt(
            l, yp, mod_p[l], w, consts, k_all, v_all, l_all)
        jlo = _first_key_block(ft, q_norm_g[l], k_norm_g[l], TQ)
        a = _attn_prompt(qt, kb, fb, vtb, jlo)
        yp = _post(l, yp, a, sga, mb, mod_p[l], w, 1, False)

        qs, kn, vn, lfn, cvs, sga_s, mb_s = _inproj_sample(l, ys, mod_s, w, consts)
        lfn_t = lfn[0, :, :N_HEADS].reshape(S, n_new, N_HEADS).transpose(0, 2, 1)
        lfn_t = jnp.pad(lfn_t, ((0, 0), (0, 0), (0, LANES - n_new)))
        a_s = _attn_sample(l, page_table, qs.reshape(S, n_new, WIDTH), kn.reshape(S, n_new, WIDTH),
                           vn.reshape(S, n_new, WIDTH), lfn_t, ck_t, cv_t, cl_t, consts)
        ys = _post(l, ys, a_s.reshape(1, R, WIDTH), sga_s, mb_s, mod_s, w, R, True)

        ks_l.append(kn.reshape(S, n_new, N_HEADS, HEAD_DIM))
        vs_l.append(vn.reshape(S, n_new, N_HEADS, HEAD_DIM))
        ls_l.append(lfn[0, :, :N_HEADS].reshape(S, n_new, N_HEADS))
        cv_l.append(cvs.reshape(S, n_new, WIDTH))

    k_prompt = k_all.reshape(DEPTH, B, N_HEADS, HEAD_DIM, T).transpose(0, 1, 4, 2, 3)
    v_prompt = v_all.reshape(DEPTH, B, N_HEADS, HEAD_DIM, T).transpose(0, 1, 4, 2, 3)
    logf_prompt = l_all.transpose(0, 1, 3, 2)
    return (yp, ys.reshape(S, n_new, D_MODEL), k_prompt, v_prompt, logf_prompt,
            jnp.stack(ks_l), jnp.stack(vs_l), jnp.stack(ls_l), jnp.stack(cv_l))
```

```python
import functools
import math

import jax
import jax.numpy as jnp
import numpy as np
from jax import lax
from jax.experimental import pallas as pl
from jax.experimental.pallas import tpu as pltpu

F32 = jnp.float32
BF16 = jnp.bfloat16

D_MODEL = 1024
DEPTH = 4
N_HEADS = 8
HEAD_DIM = 64
WIDTH = N_HEADS * HEAD_DIM
CHUNK = 128
PAGE = 128
D_FF = 2816
FF_CHUNK = 256
N_FF_CHUNKS = D_FF // FF_CHUNK
N_ADA = 6
EPS = 1e-6
NEG_INF = -1e30
SCALE = HEAD_DIM ** -0.5
LOG2E = math.log2(math.e)
ZERO_WEIGHT_EXPONENT = -105.0
LANES = 128
SUBLANES = 8
VMEM_LIMIT = 56 * 1024 * 1024

TM = 512
TQ = 256
PAIRS_PER_LOOP = 4


def _dot(a, b):
    return jnp.dot(a, b, preferred_element_type=F32)


def _dot_nt(a, b):
    return lax.dot_general(a, b, (((1,), (1,)), ((), ())), preferred_element_type=F32)


def _split3(a):
    hi = a.astype(BF16)
    r = a - hi.astype(F32)
    mid = r.astype(BF16)
    lo = (r - mid.astype(F32)).astype(BF16)
    return hi, mid, lo


def _dot3_lhs(a, b):
    hi, mid, lo = _split3(a)
    return (_dot(lo, b) + _dot(mid, b)) + _dot(hi, b)


def _dot3_rhs(a, b):
    hi, mid, lo = _split3(b)
    return (_dot(a, lo) + _dot(a, mid)) + _dot(a, hi)


def _sigmoid(x):
    return 1.0 / (1.0 + jnp.exp(-x))


def _log_sigmoid(x):
    return jnp.minimum(x, 0.0) - jnp.log1p(jnp.exp(-jnp.abs(x)))


def _gelu_tanh(x):
    c = math.sqrt(2.0 / math.pi)
    return 0.5 * x * (1.0 + jnp.tanh(c * (x + 0.044715 * (x * x * x))))


def _rms(x):
    return x * lax.rsqrt(jnp.mean(x * x, axis=-1, keepdims=True) + EPS)


def _mod_kernel(c_ref, w_ref, b_ref, o_ref):
    c = c_ref[...]
    s = (c * _sigmoid(c)).astype(BF16)
    o_ref[0, 0] = _dot(s, w_ref[0].astype(BF16)) + b_ref[0]


def _modulation(c_all, ada_w, ada_b):
    rows = c_all.shape[0]
    n_col = N_ADA * D_MODEL
    return pl.pallas_call(
        _mod_kernel,
        grid=(DEPTH, N_ADA),
        in_specs=[
            pl.BlockSpec((rows, D_MODEL), lambda l, n: (0, 0)),
            pl.BlockSpec((1, D_MODEL, D_MODEL), lambda l, n: (l, 0, n)),
            pl.BlockSpec((1, 1, D_MODEL), lambda l, n: (l, 0, n)),
        ],
        out_specs=pl.BlockSpec((1, 1, rows, D_MODEL), lambda l, n: (l, n, 0, 0)),
        out_shape=jax.ShapeDtypeStruct((DEPTH, N_ADA, rows, D_MODEL), F32),
        compiler_params=pltpu.CompilerParams(vmem_limit_bytes=VMEM_LIMIT),
        name="adaln_modulation",
    )(c_all, ada_w, ada_b.reshape(DEPTH, 1, n_col))


def _inproj_common(x, m, n1_ref, wqkv_ref, wf_ref, wzb_ref, wg_ref, bf_ref, qg_ref, kg_ref,
                   lng_ref, lnb_ref, gmat_ref, q_scale):
    sh1, sc1 = m[0], m[1]
    h = _rms(x) * n1_ref[0]
    h = h * (1.0 + sc1) + sh1
    hb = h.astype(BF16)

    qkv = _dot(hb, wqkv_ref[0])
    gmat = gmat_ref[...]

    def headnorm(z, g):
        sq = z * z
        hi = sq.astype(BF16)
        lo = (sq - hi.astype(F32)).astype(BF16)
        ms = _dot(hi, gmat) + _dot(lo, gmat)
        return z * lax.rsqrt(ms + EPS) * g

    q = headnorm(qkv[:, :WIDTH], qg_ref[0]) * q_scale
    k = headnorm(qkv[:, WIDTH:2 * WIDTH], kg_ref[0])
    v = qkv[:, 2 * WIDTH:]

    logf = _log_sigmoid(_dot(hb, wf_ref[0]) + bf_ref[0])

    zb = _gelu_tanh(_dot(hb, wzb_ref[0]))
    u = zb[:, :WIDTH]
    vb = zb[:, WIDTH:]
    mu = jnp.mean(vb, axis=-1, keepdims=True)
    xc = vb - mu
    vbn = xc * lax.rsqrt(jnp.mean(xc * xc, axis=-1, keepdims=True) + EPS) * lng_ref[0] + lnb_ref[0]

    g = _dot(hb, wg_ref[0])
    sga = _sigmoid(g[:, :D_MODEL])
    sgb = _sigmoid(g[:, D_MODEL:])
    return q, k, v, logf, u, vbn, sga, sgb


def _inproj_prompt_kernel(x_ref, mod_ref, n1_ref, wqkv_ref, wf_ref, wzb_ref, wg_ref, wbup_ref, bf_ref,
                          qg_ref, kg_ref, lng_ref, lnb_ref, gmat_ref, wpair_ref, bsp_ref, umat_ref,
                          ltri_ref, kprev_ref, vprev_ref, lprev_ref,
                          qt_ref, kb_ref, vtb_ref, ft_ref, fb_ref, sga_ref, mb_ref, kt_ref, vt_ref, lt_ref,
                          carry_ref, carry_row_ref):
    del kprev_ref, vprev_ref, lprev_ref
    tm = x_ref.shape[1]
    q, k, v, logf, u, vbn, sga, sgb = _inproj_common(
        x_ref[0], mod_ref[0], n1_ref, wqkv_ref, wf_ref, wzb_ref, wg_ref, bf_ref, qg_ref, kg_ref,
        lng_ref, lnb_ref, gmat_ref, SCALE * LOG2E)

    qt_ref[0] = q.T.astype(BF16)
    kt_ref[0, 0] = k.T
    kb_ref[0] = k.astype(BF16)
    vt = v.T
    vt_ref[0, 0] = vt
    vtb_ref[0] = vt.astype(BF16)

    lt = logf.T[:N_HEADS]
    lt_ref[0, 0] = lt

    @pl.when(pl.program_id(1) == 0)
    def _():
        carry_ref[...] = jnp.zeros_like(carry_ref)
        carry_row_ref[...] = jnp.zeros_like(carry_row_ref)

    fcum = _dot3_lhs(lt, umat_ref[...]) + carry_ref[:, 0:1]
    ft_ref[0] = fcum
    carry_ref[...] = jnp.broadcast_to(fcum[:, tm - 1:tm], carry_ref.shape)

    frow = _dot3_rhs(ltri_ref[...], logf) + carry_row_ref[0:1, :]
    carry_row_ref[...] = jnp.broadcast_to(frow[tm - 1:tm, :], carry_row_ref.shape)
    hi, mid, lo = _split3(frow * (-LOG2E))
    lane_id = lax.broadcasted_iota(jnp.int32, (1, LANES), 1)
    piece = jnp.where(lane_id < 2 * SUBLANES, hi.astype(F32),
                      jnp.where(lane_id < 4 * SUBLANES, mid.astype(F32), lo.astype(F32)))
    used = (lane_id % (2 * SUBLANES) < N_HEADS) & (lane_id < 6 * SUBLANES)
    fb_ref[0] = jnp.where(used, piece, 0.0).astype(BF16)

    lane = lax.broadcasted_iota(jnp.int32, (1, LANES), 1)
    left = lane < HEAD_DIM
    chunks = []
    for c in range(tm // CHUNK):
        xc = vbn[c * CHUNK:(c + 1) * CHUNK]
        cols = []
        for jp in range(WIDTH // LANES):
            xp = xc[:, jp * LANES:(jp + 1) * LANES]
            xs = jnp.concatenate([jnp.where(left, xp, 0.0), jnp.where(left, 0.0, xp)], axis=0)
            cols.append(_dot(wpair_ref[0, jp], xs.astype(BF16)))
        chunks.append(jnp.concatenate(cols, axis=1) + bsp_ref[0])
    s = jnp.concatenate(chunks, axis=0)
    bg = (u * s).astype(BF16)

    sga_ref[0] = sga.astype(BF16)
    mb_ref[0] = (sgb * _dot(bg, wbup_ref[0])).astype(BF16)


def _inproj_sample_kernel(x_ref, mod_ref, n1_ref, wqkv_ref, wf_ref, wzb_ref, wg_ref, wbup_ref, bf_ref,
                          qg_ref, kg_ref, lng_ref, lnb_ref, gmat_ref, coef_ref, bsp_ref,
                          q_ref, k_ref, v_ref, lf_ref, cv_ref, sga_ref, mb_ref):
    q, k, v, logf, u, vbn, sga, sgb = _inproj_common(
        x_ref[0], mod_ref[0], n1_ref, wqkv_ref, wf_ref, wzb_ref, wg_ref, bf_ref, qg_ref, kg_ref,
        lng_ref, lnb_ref, gmat_ref, SCALE)
    q_ref[0] = q
    k_ref[0] = k
    v_ref[0] = v
    lf_ref[0] = logf
    cv_ref[0] = vbn

    n_rows = vbn.shape[0]
    groups = lambda a: a.reshape(n_rows // SUBLANES, SUBLANES, WIDTH)
    s = bsp_ref[0][None] + coef_ref[0, 0][None] * groups(vbn)
    for d in range(1, coef_ref.shape[1]):
        s = s + coef_ref[0, d][None] * groups(pltpu.roll(vbn, d, axis=0))
    bg = (u * s.reshape(n_rows, WIDTH)).astype(BF16)
    sga_ref[0] = sga.astype(BF16)
    mb_ref[0] = (sgb * _dot(bg, wbup_ref[0])).astype(BF16)


def _layer_spec(tail, l, nidx):
    zeros = (0,) * len(tail)
    if nidx == 1:
        imap = lambda i: (l,) + zeros
    else:
        imap = lambda b, i: (l,) + zeros
    return pl.BlockSpec((1,) + tuple(tail), imap, pipeline_mode=pl.Buffered(1))


def _const_spec(shape, nidx):
    zeros = (0,) * len(shape)
    if nidx == 1:
        imap = lambda i: zeros
    else:
        imap = lambda b, i: zeros
    return pl.BlockSpec(tuple(shape), imap, pipeline_mode=pl.Buffered(1))


def _inproj_prompt(l, x, mod, w, consts, k_all, v_all, l_all):
    B, T, _ = x.shape
    tm = TM
    grid = (B, T // tm)
    row = lambda width: pl.BlockSpec((1, tm, width), lambda b, i: (b, i, 0))
    col = lambda height: pl.BlockSpec((1, height, tm), lambda b, i: (b, 0, i))
    stacked = lambda height: pl.BlockSpec((1, 1, height, tm), lambda b, i: (l, b, 0, i))
    in_specs = [
        row(D_MODEL),
        pl.BlockSpec((1, N_ADA, 1, D_MODEL), lambda b, i: (b, 0, 0, 0)),
        _layer_spec((1, D_MODEL), l, 2),
        _layer_spec((D_MODEL, 3 * WIDTH), l, 2),
        _layer_spec((D_MODEL, LANES), l, 2),
        _layer_spec((D_MODEL, 2 * WIDTH), l, 2),
        _layer_spec((D_MODEL, 2 * D_MODEL), l, 2),
        _layer_spec((WIDTH, D_MODEL), l, 2),
        _layer_spec((1, LANES), l, 2),
        _layer_spec((1, WIDTH), l, 2),
        _layer_spec((1, WIDTH), l, 2),
        _layer_spec((1, WIDTH), l, 2),
        _layer_spec((1, WIDTH), l, 2),
        _const_spec((WIDTH, WIDTH), 2),
        _layer_spec((WIDTH // LANES, CHUNK, 2 * CHUNK), l, 2),
        _layer_spec((CHUNK, WIDTH), l, 2),
        _const_spec((tm, tm), 2),
        _const_spec((tm, tm), 2),
        pl.BlockSpec(memory_space=pl.ANY),
        pl.BlockSpec(memory_space=pl.ANY),
        pl.BlockSpec(memory_space=pl.ANY),
    ]
    out_specs = [col(WIDTH), row(WIDTH), col(WIDTH), col(N_HEADS), row(LANES), row(D_MODEL),
                 row(D_MODEL), stacked(WIDTH), stacked(WIDTH), stacked(N_HEADS)]
    out_shape = [
        jax.ShapeDtypeStruct((B, WIDTH, T), BF16),
        jax.ShapeDtypeStruct((B, T, WIDTH), BF16),
        jax.ShapeDtypeStruct((B, WIDTH, T), BF16),
        jax.ShapeDtypeStruct((B, N_HEADS, T), F32),
        jax.ShapeDtypeStruct((B, T, LANES), BF16),
        jax.ShapeDtypeStruct((B, T, D_MODEL), BF16),
        jax.ShapeDtypeStruct((B, T, D_MODEL), BF16),
        jax.ShapeDtypeStruct(k_all.shape, F32),
        jax.ShapeDtypeStruct(v_all.shape, F32),
        jax.ShapeDtypeStruct(l_all.shape, F32),
    ]
    n_in = len(in_specs)
    return pl.pallas_call(
        _inproj_prompt_kernel,
        grid=grid,
        in_specs=in_specs,
        out_specs=out_specs,
        out_shape=out_shape,
        scratch_shapes=[pltpu.VMEM((N_HEADS, LANES), F32), pltpu.VMEM((SUBLANES, LANES), F32)],
        input_output_aliases={n_in - 3: 7, n_in - 2: 8, n_in - 1: 9},
        compiler_params=pltpu.CompilerParams(
            dimension_semantics=("arbitrary", "arbitrary"), vmem_limit_bytes=VMEM_LIMIT),
        name="inproj_prompt",
    )(x, mod, w["n1"], w["wqkv"], w["wf"], w["wzb"], w["wg"], w["wbup"], w["bf"], w["qg"], w["kg"],
      w["lng"], w["lnb"], consts["gmat"], w["wpair"], w["bsp"], consts["umat"], consts["ltri"],
      k_all, v_all, l_all)


def _inproj_sample(l, x, mod, w, consts):
    _, R, _ = x.shape
    full = lambda width: pl.BlockSpec((1, R, width), lambda i: (0, 0, 0))
    in_specs = [
        full(D_MODEL),
        pl.BlockSpec((1, N_ADA, R, D_MODEL), lambda i: (l, 0, 0, 0), pipeline_mode=pl.Buffered(1)),
        _layer_spec((1, D_MODEL), l, 1),
        _layer_spec((D_MODEL, 3 * WIDTH), l, 1),
        _layer_spec((D_MODEL, LANES), l, 1),
        _layer_spec((D_MODEL, 2 * WIDTH), l, 1),
        _layer_spec((D_MODEL, 2 * D_MODEL), l, 1),
        _layer_spec((WIDTH, D_MODEL), l, 1),
        _layer_spec((1, LANES), l, 1),
        _layer_spec((1, WIDTH), l, 1),
        _layer_spec((1, WIDTH), l, 1),
        _layer_spec((1, WIDTH), l, 1),
        _layer_spec((1, WIDTH), l, 1),
        _const_spec((WIDTH, WIDTH), 1),
        _layer_spec(w["coef"].shape[1:], l, 1),
        _layer_spec((SUBLANES, WIDTH), l, 1),
    ]
    out_specs = [full(WIDTH), full(WIDTH), full(WIDTH), full(LANES), full(WIDTH), full(D_MODEL),
                 full(D_MODEL)]
    out_shape = [
        jax.ShapeDtypeStruct((1, R, WIDTH), F32),
        jax.ShapeDtypeStruct((1, R, WIDTH), F32),
        jax.ShapeDtypeStruct((1, R, WIDTH), F32),
        jax.ShapeDtypeStruct((1, R, LANES), F32),
        jax.ShapeDtypeStruct((1, R, WIDTH), F32),
        jax.ShapeDtypeStruct((1, R, D_MODEL), BF16),
        jax.ShapeDtypeStruct((1, R, D_MODEL), BF16),
    ]
    return pl.pallas_call(
        _inproj_sample_kernel,
        grid=(1,),
        in_specs=in_specs,
        out_specs=out_specs,
        out_shape=out_shape,
        compiler_params=pltpu.CompilerParams(vmem_limit_bytes=VMEM_LIMIT),
        name="inproj_sample",
    )(x, mod, w["n1"], w["wqkv"], w["wf"], w["wzb"], w["wg"], w["wbup"], w["bf"], w["qg"], w["kg"],
      w["lng"], w["lnb"], consts["gmat"], w["coef"], w["bsp_s"])


def _attn_prompt_kernel(jlo_ref, qt_ref, k_ref, fb_ref, vt_ref, o_ref, qa_sc, m_sc, acc_sc):
    b = pl.program_id(0)
    i = pl.program_id(1)
    tq = qt_ref.shape[2]
    sub = lax.broadcasted_iota(jnp.int32, (LANES, tq), 0)
    key_ix = lax.broadcasted_iota(jnp.int32, (tq, tq), 0)
    qry_ix = lax.broadcasted_iota(jnp.int32, (tq, tq), 1)

    keeps, ones_rows = [], []
    for half in range(2):
        mine = (sub // HEAD_DIM) == half
        keeps.append(jnp.where(mine, 1.0, 0.0).astype(BF16))
        l_row = HEAD_DIM if half == 0 else 0
        ones_rows.append(jnp.where(sub == l_row, 1.0, 0.0).astype(BF16))

    n_pairs = WIDTH // LANES
    for g in range(n_pairs // PAIRS_PER_LOOP):
        pairs = range(g * PAIRS_PER_LOOP, (g + 1) * PAIRS_PER_LOOP)
        jlo = i
        for jp in pairs:
            qtp = qt_ref[0, jp * LANES:(jp + 1) * LANES, :].astype(F32)
            for half in range(2):
                h = 2 * jp + half
                slot = h - 2 * pairs[0]
                mine = (sub // HEAD_DIM) == half
                sel = (sub == h) | (sub == 2 * SUBLANES + h) | (sub == 4 * SUBLANES + h)
                qa_sc[slot] = jnp.concatenate(
                    [jnp.where(mine, qtp, 0.0), jnp.where(sel, 1.0, 0.0)], axis=0).astype(BF16)
                m_sc[slot] = jnp.full((1, tq), NEG_INF, F32)
                acc_sc[slot] = jnp.zeros((LANES, tq), F32)
                jlo = jnp.minimum(jlo, jlo_ref[b, i, h])

        def scores_of(j, pairs=pairs):
            off = pl.multiple_of(j * tq, tq)
            fbj = fb_ref[0, pl.ds(off, tq), :]
            scores = []
            for jp in pairs:
                ka = jnp.concatenate(
                    [k_ref[0, pl.ds(off, tq), jp * LANES:(jp + 1) * LANES], fbj], axis=1)
                for half in range(2):
                    scores.append(_dot(ka, qa_sc[2 * (jp - pairs[0]) + half]))
            return tuple(scores)

        def softmax_pv(j, scores, diagonal, pairs=pairs):
            off = pl.multiple_of(j * tq, tq)
            probs, alphas = [], []
            for slot, s in enumerate(scores):
                if diagonal:
                    s = jnp.where(key_ix <= qry_ix, s, NEG_INF)
                m_prev = m_sc[slot]
                m_new = jnp.maximum(m_prev, jnp.max(s, axis=0, keepdims=True))
                alphas.append(jnp.exp2(m_prev - m_new))
                probs.append(jnp.exp2(s - m_new).astype(BF16))
                m_sc[slot] = m_new
            for jp in pairs:
                vt = vt_ref[0, jp * LANES:(jp + 1) * LANES, pl.ds(off, tq)]
                for half in range(2):
                    slot = 2 * (jp - pairs[0]) + half
                    va = vt * keeps[half] + ones_rows[half]
                    acc_sc[slot] = alphas[slot] * acc_sc[slot] + _dot(va, probs[slot])

        n_off = i - jlo
        odd = lax.rem(n_off, 2)

        @pl.when(odd == 1)
        def _():
            softmax_pv(jlo, scores_of(jlo), False)

        def two_blocks(t, carry, start=jlo + odd):
            j = start + 2 * t
            s0 = scores_of(j)
            s1 = scores_of(j + 1)
            softmax_pv(j, s0, False)
            softmax_pv(j + 1, s1, False)
            return carry

        lax.fori_loop(0, n_off // 2, two_blocks, 0)
        softmax_pv(i, scores_of(i), True)

        for jp in pairs:
            a0 = acc_sc[2 * (jp - pairs[0])]
            a1 = acc_sc[2 * (jp - pairs[0]) + 1]
            out_t = jnp.concatenate(
                [a0[:HEAD_DIM] * (1.0 / a0[HEAD_DIM:HEAD_DIM + 1]), a1[HEAD_DIM:] * (1.0 / a1[0:1])],
                axis=0)
            o_ref[0, :, jp * LANES:(jp + 1) * LANES] = out_t.T.astype(o_ref.dtype)


def _first_key_block(ft, qg, kg, tq):
    B, H, T = ft.shape
    n = T // tq
    qk_bound = HEAD_DIM * SCALE * jnp.max(jnp.abs(qg)) * jnp.max(jnp.abs(kg))
    f_first = ft[:, :, 0::tq]
    f_last = ft[:, :, tq - 1::tq]
    exponent = 2.0 * qk_bound + f_first[:, :, :, None] - f_last[:, :, None, :]
    earlier = jnp.arange(n)[None, :] < jnp.arange(n)[:, None]
    dead = (exponent < ZERO_WEIGHT_EXPONENT) & earlier[None, None]
    jlo = jnp.min(jnp.where(dead, n, jnp.arange(n, dtype=jnp.int32)), axis=-1)
    return jlo.transpose(0, 2, 1).astype(jnp.int32)


def _attn_prompt(qt, kb, fb, vtb, jlo):
    B, T, _ = kb.shape
    tq = TQ
    resident = lambda shape: pl.BlockSpec(shape, lambda b, i, jl: (b, 0, 0), pipeline_mode=pl.Buffered(1))
    grid_spec = pltpu.PrefetchScalarGridSpec(
        num_scalar_prefetch=1,
        grid=(B, T // tq),
        in_specs=[
            pl.BlockSpec((1, WIDTH, tq), lambda b, i, jl: (b, 0, i)),
            resident((1, T, WIDTH)),
            resident((1, T, LANES)),
            resident((1, WIDTH, T)),
        ],
        out_specs=pl.BlockSpec((1, tq, WIDTH), lambda b, i, jl: (b, i, 0)),
        scratch_shapes=[
            pltpu.VMEM((2 * PAIRS_PER_LOOP, 2 * LANES, tq), BF16),
            pltpu.VMEM((2 * PAIRS_PER_LOOP, 1, tq), F32),
            pltpu.VMEM((2 * PAIRS_PER_LOOP, LANES, tq), F32),
        ],
    )
    return pl.pallas_call(
        _attn_prompt_kernel,
        grid_spec=grid_spec,
        out_shape=jax.ShapeDtypeStruct((B, T, WIDTH), BF16),
        compiler_params=pltpu.CompilerParams(
            dimension_semantics=("arbitrary", "arbitrary"), vmem_limit_bytes=VMEM_LIMIT),
        name="attn_prompt",
    )(jlo, qt, kb, fb, vtb)


def _attn_sample_kernel(l, n_pages, pt_ref, q_ref, kn_ref, vn_ref, lfn_ref, ck_ref, cv_ref, cl_ref,
                        sl_ref, ones_ref, lsuf_ref, o_ref, kbuf, vbuf, lbuf, sem):
    b = pl.program_id(0)
    nb = pl.num_programs(0)
    slot = lax.rem(b, 2)
    n_new = q_ref.shape[1]

    def page_copies(seq, sl):
        cps = []
        for p in range(n_pages):
            page = pt_ref[seq, p]
            cps.append(pltpu.make_async_copy(ck_ref.at[l, page], kbuf.at[sl, p], sem.at[0, sl]))
            cps.append(pltpu.make_async_copy(cv_ref.at[l, page], vbuf.at[sl, p], sem.at[1, sl]))
            cps.append(pltpu.make_async_copy(cl_ref.at[l, page], lbuf.at[sl, p], sem.at[2, sl]))
        return cps

    @pl.when(b == 0)
    def _():
        for cp in page_copies(0, 0):
            cp.start()

    @pl.when(b + 1 < nb)
    def _():
        for cp in page_copies(b + 1, 1 - slot):
            cp.start()

    for cp in page_copies(b, slot):
        cp.wait()

    bd = (lax.broadcasted_iota(jnp.int32, (N_HEADS, WIDTH), 0)
          == lax.broadcasted_iota(jnp.int32, (N_HEADS, WIDTH), 1) // HEAD_DIM)

    q4 = q_ref[0]
    qe = jnp.concatenate(
        [jnp.where(bd, jnp.broadcast_to(q4[t:t + 1, :], (N_HEADS, WIDTH)), 0.0) for t in range(n_new)],
        axis=0)
    qeb = qe.astype(BF16)
    n_rows = n_new * N_HEADS

    lfn = lfn_ref[0]
    c = [lfn[:, 0:1]]
    for t in range(1, n_new):
        c.append(c[-1] + lfn[:, t:t + 1])
    cq = jnp.concatenate(c, axis=0)

    lf = lbuf[slot].reshape(n_pages * N_HEADS, PAGE)
    within = _dot3_lhs(lf, sl_ref[...])
    totals = _dot3_lhs(lf, ones_ref[...])
    decay = within + _dot3_rhs(lsuf_ref[...], totals)

    s_pages = []
    for p in range(n_pages):
        sp = _dot(qeb, kbuf[slot, p].astype(BF16))
        dp = decay[p * N_HEADS:(p + 1) * N_HEADS]
        bias = jnp.concatenate([c[t] + dp for t in range(n_new)], axis=0)
        s_pages.append(sp + bias)

    lane = lax.broadcasted_iota(jnp.int32, (n_rows, LANES), 1)
    row_t = lax.broadcasted_iota(jnp.int32, (n_rows, LANES), 0) // N_HEADS
    kn = kn_ref[0]
    vn = vn_ref[0]
    s_new = jnp.full((n_rows, LANES), NEG_INF, F32)
    for t2 in range(n_new):
        col = jnp.sum(qe * kn[t2:t2 + 1, :], axis=-1, keepdims=True)
        ck = jnp.concatenate([c[t2]] * n_new, axis=0)
        val = col + cq - ck
        s_new = jnp.where((lane == t2) & (row_t >= t2), val, s_new)

    m = jnp.max(s_new, axis=-1, keepdims=True)
    for sp in s_pages:
        m = jnp.maximum(m, jnp.max(sp, axis=-1, keepdims=True))

    p_new = jnp.exp(s_new - m)
    lsum = jnp.sum(p_new, axis=-1, keepdims=True)
    o = jnp.zeros((n_rows, WIDTH), F32)
    for t2 in range(n_new):
        o = o + p_new[:, t2:t2 + 1] * vn[t2:t2 + 1, :]
    for p in range(n_pages):
        pp = jnp.exp(s_pages[p] - m)
        lsum = lsum + jnp.sum(pp, axis=-1, keepdims=True)
        o = o + _dot_nt(pp.astype(BF16), vbuf[slot, p].astype(BF16))
    o = o / lsum

    outs = []
    for t in range(n_new):
        ot = jnp.where(bd, o[t * N_HEADS:(t + 1) * N_HEADS], 0.0)
        outs.append(jnp.sum(ot, axis=0, keepdims=True))
    o_ref[0] = jnp.concatenate(outs, axis=0)


def _attn_sample(l, page_table, q, kn, vn, lfn_t, ck_t, cv_t, cl_t, consts):
    S, n_new, _ = q.shape
    n_pages = page_table.shape[1]
    per_seq = lambda shape: pl.BlockSpec((1,) + shape, lambda b, pt: (b, 0, 0))
    const = lambda shape: pl.BlockSpec(shape, lambda b, pt: (0, 0), pipeline_mode=pl.Buffered(1))
    n_ph = n_pages * N_HEADS
    grid_spec = pltpu.PrefetchScalarGridSpec(
        num_scalar_prefetch=1,
        grid=(S,),
        in_specs=[
            per_seq((n_new, WIDTH)), per_seq((n_new, WIDTH)), per_seq((n_new, WIDTH)),
            per_seq((N_HEADS, LANES)),
            pl.BlockSpec(memory_space=pl.ANY), pl.BlockSpec(memory_space=pl.ANY),
            pl.BlockSpec(memory_space=pl.ANY),
            const((PAGE, PAGE)), const((PAGE, PAGE)), const((n_ph, n_ph)),
        ],
        out_specs=per_seq((n_new, WIDTH)),
        scratch_shapes=[
            pltpu.VMEM((2, n_pages, WIDTH, PAGE), F32),
            pltpu.VMEM((2, n_pages, WIDTH, PAGE), F32),
            pltpu.VMEM((2, n_pages, N_HEADS, PAGE), F32),
            pltpu.SemaphoreType.DMA((3, 2)),
        ],
    )
    return pl.pallas_call(
        functools.partial(_attn_sample_kernel, l, n_pages),
        grid_spec=grid_spec,
        out_shape=jax.ShapeDtypeStruct((S, n_new, WIDTH), F32),
        compiler_params=pltpu.CompilerParams(
            dimension_semantics=("arbitrary",), vmem_limit_bytes=VMEM_LIMIT),
        name="attn_sample",
    )(page_table, q, kn, vn, lfn_t, ck_t, cv_t, cl_t, consts["sl"], consts["ones"], consts["lsuf"])


def _post_kernel(x_ref, a_ref, sga_ref, mb_ref, mod_ref, n2_ref, waup_ref, wout_ref, w1_ref, w2_ref,
                 o_ref, acc_ref, h2_ref):
    m = mod_ref[0]
    g1, sh2, sc2, g2 = m[2], m[3], m[4], m[5]
    au = _dot(a_ref[0].astype(BF16), waup_ref[0])
    merged = sga_ref[0].astype(F32) * au + mb_ref[0].astype(F32)
    x1 = x_ref[0] + g1 * _dot(merged.astype(BF16), wout_ref[0])
    o_ref[0] = x1
    h2 = _rms(x1) * n2_ref[0]
    h2_ref[...] = (h2 * (1.0 + sc2) + sh2).astype(BF16)
    acc_ref[...] = jnp.zeros_like(acc_ref)

    def ffn_chunk(c, carry):
        col = pl.multiple_of(c * FF_CHUNK, FF_CHUNK)
        h2 = h2_ref[...]
        gate = _dot(h2, w1_ref[0, :, pl.ds(col, FF_CHUNK)])
        up = _dot(h2, w1_ref[0, :, pl.ds(pl.multiple_of(D_FF + col, FF_CHUNK), FF_CHUNK)])
        act = (gate * _sigmoid(gate) * up).astype(BF16)
        acc_ref[...] += _dot(act, w2_ref[0, c])
        return carry

    lax.fori_loop(0, N_FF_CHUNKS, ffn_chunk, 0, unroll=True)
    o_ref[0] = o_ref[0] + g2 * acc_ref[...]


def _post(l, x, a, sga, mb, mod, w, mod_rows, mod_layer_indexed):
    B, T, _ = x.shape
    tm = min(TM, T)
    grid = (B, T // tm)
    row = lambda width: pl.BlockSpec((1, tm, width), lambda b, i: (b, i, 0))
    if mod_layer_indexed:
        mod_spec = pl.BlockSpec((1, N_ADA, mod_rows, D_MODEL), lambda b, i: (l, 0, 0, 0),
                                pipeline_mode=pl.Buffered(1))
    else:
        mod_spec = pl.BlockSpec((1, N_ADA, mod_rows, D_MODEL), lambda b, i: (b, 0, 0, 0))
    return pl.pallas_call(
        _post_kernel,
        grid=grid,
        in_specs=[
            row(D_MODEL), row(WIDTH), row(D_MODEL), row(D_MODEL), mod_spec,
            _layer_spec((1, D_MODEL), l, 2),
            _layer_spec((WIDTH, D_MODEL), l, 2),
            _layer_spec((D_MODEL, D_MODEL), l, 2),
            _layer_spec((D_MODEL, 2 * D_FF), l, 2),
            _layer_spec((N_FF_CHUNKS, FF_CHUNK, D_MODEL), l, 2),
        ],
        out_specs=row(D_MODEL),
        out_shape=jax.ShapeDtypeStruct((B, T, D_MODEL), F32),
        scratch_shapes=[pltpu.VMEM((tm, D_MODEL), F32), pltpu.VMEM((tm, D_MODEL), BF16)],
        compiler_params=pltpu.CompilerParams(
            dimension_semantics=("arbitrary", "arbitrary"), vmem_limit_bytes=VMEM_LIMIT),
        name="post",
    )(x, a, sga, mb, mod, w["n2"], w["waup"], w["wout"], w["w1"], w["w2"])


def _prepare_weights(norm1_g, norm2_g, w_in, b_f, q_norm_g, k_norm_g, v_ln_g, v_ln_b, w_spatial,
                     b_spatial, w_a_up, w_b_up, w_out, w_ffn_in, w_ffn_out, n_new):
    c0 = 3 * WIDTH
    c1 = c0 + N_HEADS
    c2 = c1 + 2 * WIDTH
    w = {}
    w["n1"] = norm1_g.reshape(DEPTH, 1, D_MODEL)
    w["n2"] = norm2_g.reshape(DEPTH, 1, D_MODEL)
    w_in = w_in.astype(BF16)
    w["wqkv"] = w_in[:, :, :c0]
    wf8 = jnp.pad(w_in[:, :, c0:c1], ((0, 0), (0, 0), (0, 2 * SUBLANES - N_HEADS)))
    w["wf"] = jnp.pad(jnp.tile(wf8, (1, 1, 3)), ((0, 0), (0, 0), (0, LANES - 6 * SUBLANES)))
    w["wzb"] = w_in[:, :, c1:c2]
    w["wg"] = w_in[:, :, c2:]
    bf8 = jnp.pad(b_f, ((0, 0), (0, 2 * SUBLANES - N_HEADS)))
    w["bf"] = jnp.pad(jnp.tile(bf8, (1, 3)), ((0, 0), (0, LANES - 6 * SUBLANES))).reshape(DEPTH, 1, LANES)
    w["qg"] = jnp.tile(q_norm_g, (1, N_HEADS)).reshape(DEPTH, 1, WIDTH)
    w["kg"] = jnp.tile(k_norm_g, (1, N_HEADS)).reshape(DEPTH, 1, WIDTH)
    w["lng"] = v_ln_g.reshape(DEPTH, 1, WIDTH)
    w["lnb"] = v_ln_b.reshape(DEPTH, 1, WIDTH)
    w["waup"] = w_a_up.astype(BF16)
    w["wbup"] = w_b_up.astype(BF16)
    w["wout"] = w_out.astype(BF16)
    w["w1"] = w_ffn_in.astype(BF16)
    w["w2"] = w_ffn_out.reshape(DEPTH, N_FF_CHUNKS, FF_CHUNK, D_MODEL).astype(BF16)
    wt = jnp.tril(w_spatial)
    wt = wt.reshape(DEPTH, WIDTH // LANES, 2, CHUNK, CHUNK)
    w["wpair"] = jnp.concatenate([wt[:, :, 0], wt[:, :, 1]], axis=-1).astype(BF16)
    w["bsp"] = jnp.repeat(b_spatial.transpose(0, 2, 1), HEAD_DIM, axis=2)
    assert SUBLANES % n_new == 0
    ws = jnp.tril(w_spatial[:, :, :n_new, :n_new])
    coefs = []
    for d in range(n_new):
        diag = jnp.stack([ws[:, :, t, t - d] if t >= d else jnp.zeros_like(ws[:, :, 0, 0])
                          for t in range(n_new)], axis=1)
        coefs.append(jnp.tile(jnp.repeat(diag, HEAD_DIM, axis=2), (1, SUBLANES // n_new, 1)))
    w["coef"] = jnp.stack(coefs, axis=1)
    bs = jnp.repeat(b_spatial[:, :, :n_new].transpose(0, 2, 1), HEAD_DIM, axis=2)
    w["bsp_s"] = jnp.tile(bs, (1, SUBLANES // n_new, 1))
    return w


def _constants(n_pages):
    r = np.arange(WIDTH)
    gmat = np.where((r[:, None] // HEAD_DIM) == (r[None, :] // HEAD_DIM), 1.0 / HEAD_DIM, 0.0)
    t = np.arange(TM)
    umat = (t[:, None] <= t[None, :])
    p = np.arange(PAGE)
    sl = (p[:, None] > p[None, :])
    ph = np.arange(n_pages * N_HEADS)
    lsuf = ((ph[:, None] % N_HEADS) == (ph[None, :] % N_HEADS)) & (
        (ph[None, :] // N_HEADS) > (ph[:, None] // N_HEADS))
    as_bf16 = lambda a: jnp.asarray(np.asarray(a, np.float32), BF16)
    return {
        "gmat": as_bf16(gmat),
        "umat": as_bf16(umat),
        "ltri": as_bf16(umat.T),
        "sl": as_bf16(sl),
        "ones": as_bf16(np.ones((PAGE, PAGE))),
        "lsuf": as_bf16(lsuf),
    }


def kernel(x_prompt, x_sample, cache_k, cache_v, cache_logf, page_table, c_prompt, c_sample, ada_w, ada_b, norm1_g, norm2_g, w_in, b_f, q_norm_g, k_norm_g, v_ln_g, v_ln_b, w_spatial, b_spatial, w_a_up, w_b_up, w_out, w_ffn_in, w_ffn_out):
    B, T, _ = x_prompt.shape
    S, n_new, _ = x_sample.shape
    n_pages = page_table.shape[1]
    R = S * n_new

    w = _prepare_weights(norm1_g, norm2_g, w_in, b_f, q_norm_g, k_norm_g, v_ln_g, v_ln_b, w_spatial,
                         b_spatial, w_a_up, w_b_up, w_out, w_ffn_in, w_ffn_out, n_new)
    consts = _constants(n_pages)

    pad = (-(R + B)) % SUBLANES
    c_all = jnp.concatenate(
        [jnp.repeat(c_sample, n_new, axis=0), c_prompt, jnp.zeros((pad, D_MODEL), F32)], axis=0)
    mod_s = _modulation(c_all, ada_w, ada_b)
    mod_p = mod_s[:, :, R:R + B].transpose(0, 2, 1, 3).reshape(DEPTH, B, N_ADA, 1, D_MODEL)

    ck_t = cache_k.transpose(0, 1, 3, 4, 2).reshape(DEPTH, -1, WIDTH, PAGE)
    cv_t = cache_v.transpose(0, 1, 3, 4, 2).reshape(DEPTH, -1, WIDTH, PAGE)
    cl_t = cache_logf.transpose(0, 1, 3, 2)

    k_all = jnp.zeros((DEPTH, B, WIDTH, T), F32)
    v_all = jnp.zeros((DEPTH, B, WIDTH, T), F32)
    l_all = jnp.zeros((DEPTH, B, N_HEADS, T), F32)

    yp = x_prompt
    ys = x_sample.reshape(1, R, D_MODEL)
    ks_l, vs_l, ls_l, cv_l = [], [], [], []
    for l in range(DEPTH):
        qt, kb, vtb, ft, fb, sga, mb, k_all, v_all, l_all = _inproj_prompt(
            l, yp, mod_p[l], w, consts, k_all, v_all, l_all)
        jlo = _first_key_block(ft, q_norm_g[l], k_norm_g[l], TQ)
        a = _attn_prompt(qt, kb, fb, vtb, jlo)
        yp = _post(l, yp, a, sga, mb, mod_p[l], w, 1, False)

        qs, kn, vn, lfn, cvs, sga_s, mb_s = _inproj_sample(l, ys, mod_s, w, consts)
        lfn_t = lfn[0, :, :N_HEADS].reshape(S, n_new, N_HEADS).transpose(0, 2, 1)
        lfn_t = jnp.pad(lfn_t, ((0, 0), (0, 0), (0, LANES - n_new)))
        a_s = _attn_sample(l, page_table, qs.reshape(S, n_new, WIDTH), kn.reshape(S, n_new, WIDTH),
                           vn.reshape(S, n_new, WIDTH), lfn_t, ck_t, cv_t, cl_t, consts)
        ys = _post(l, ys, a_s.reshape(1, R, WIDTH), sga_s, mb_s, mod_s, w, R, True)

        ks_l.append(kn.reshape(S, n_new, N_HEADS, HEAD_DIM))
        vs_l.append(vn.reshape(S, n_new, N_HEADS, HEAD_DIM))
        ls_l.append(lfn[0, :, :N_HEADS].reshape(S, n_new, N_HEADS))
        cv_l.append(cvs.reshape(S, n_new, WIDTH))

    k_prompt = k_all.reshape(DEPTH, B, N_HEADS, HEAD_DIM, T).transpose(0, 1, 4, 2, 3)
    v_prompt = v_all.reshape(DEPTH, B, N_HEADS, HEAD_DIM, T).transpose(0, 1, 4, 2, 3)
    logf_prompt = l_all.transpose(0, 1, 3, 2)
    return (yp, ys.reshape(S, n_new, D_MODEL), k_prompt, v_prompt, logf_prompt,
            jnp.stack(ks_l), jnp.stack(vs_l), jnp.stack(ls_l), jnp.stack(cv_l))
```
